```python
import math
import jax, jax.numpy as jnp
from jax import lax
import numpy as np

D_MODEL = 1024
BATCH = 4
SEQ = 4096
DEPTH = 4

N_Q_HEADS = 16
N_KV_HEADS = 2
HEAD_DIM = 64
Q_GROUP = N_Q_HEADS // N_KV_HEADS
WINDOW = 128
BLOCK = 128
ATTN_W = N_Q_HEADS * HEAD_DIM
KV_W = N_KV_HEADS * HEAD_DIM
N_BUCKETS = 32
MAX_DISTANCE = 128
LRU_W = D_MODEL
LRU_BLOCKS = 8
LRU_BW = LRU_W // LRU_BLOCKS
CONV_W = 4
LRU_C = 8.0
N_EXPERTS = 16
N_GROUPS = 4
EXPERTS_PER_GROUP = N_EXPERTS // N_GROUPS
TOP_K = 2
EXPERT_FF = 512
ALPHA = (2 * DEPTH) ** 0.25
BETA = (8 * DEPTH) ** -0.25
LN_EPS = 1e-5
IN_SPLITS = (ATTN_W, KV_W, KV_W, LRU_W, LRU_W, D_MODEL, D_MODEL)
IN_W = sum(IN_SPLITS)

kernel_name = "hybrid_swa_rglru_grouped_moe_deepnorm_adaln"


def _layer_norm(x, g, b):
    xf = x.astype(jnp.float32)
    mu = xf.mean(-1, keepdims=True)
    var = jnp.square(xf - mu).mean(-1, keepdims=True)
    return ((xf - mu) * lax.rsqrt(var + LN_EPS) * g + b).astype(x.dtype)


def _t5_causal_bucket(dist):
    max_exact = N_BUCKETS // 2
    d = np.maximum(dist, 0)
    df = np.maximum(d, 1).astype(np.float32)
    large = max_exact + (np.log(df / max_exact) / math.log(MAX_DISTANCE / max_exact)
                         * (N_BUCKETS - max_exact)).astype(np.int32)
    large = np.minimum(large, N_BUCKETS - 1)
    return np.where(d < max_exact, d, large).astype(np.int32)


def _sliding_window_attention(q, k, v, rel_bias, sink):
    b, s, _ = q.shape
    nb = s // BLOCK
    q = q.reshape(b, nb, BLOCK, N_KV_HEADS, Q_GROUP, HEAD_DIM)

    def band(t):
        t = t.reshape(b, nb, BLOCK, N_KV_HEADS, HEAD_DIM)
        prev = jnp.pad(t[:, :-1], ((0, 0), (1, 0), (0, 0), (0, 0), (0, 0)))
        return jnp.concatenate([prev, t], axis=2)

    kb, vb = band(k), band(v)
    logits = jnp.einsum('bnqkgd,bnskd->bnkgqs', q, kb,
                        preferred_element_type=jnp.float32) * (HEAD_DIM ** -0.5)
    qi = np.arange(BLOCK)[:, None]
    sj = np.arange(2 * BLOCK)[None, :]
    dist = qi + BLOCK - sj
    in_window = (dist >= 0) & (dist < WINDOW)
    key_valid = (np.arange(nb)[:, None] * BLOCK - BLOCK + sj) >= 0
    mask = in_window[None] & key_valid[:, None, :]
    bias = rel_bias.astype(jnp.float32)[_t5_causal_bucket(dist)]
    bias = jnp.transpose(bias, (2, 0, 1)).reshape(N_KV_HEADS, Q_GROUP, BLOCK, 2 * BLOCK)
    logits = jnp.where(mask[None, :, None, None], logits + bias, -jnp.inf)
    sink_l = sink.astype(jnp.float32).reshape(N_KV_HEADS, Q_GROUP, 1, 1)
    m = jnp.maximum(logits.max(-1, keepdims=True), sink_l)
    p = jnp.exp(logits - m)
    probs = p / (p.sum(-1, keepdims=True) + jnp.exp(sink_l - m))
    out = jnp.einsum('bnkgqs,bnskd->bnqkgd', probs.astype(v.dtype), vb)
    return out.reshape(b, s, ATTN_W)


def _rg_lru(xb, conv_w, conv_b, wa, ba, wx, bx, lam):
    b, s, w = xb.shape
    xp = jnp.pad(xb, ((0, 0), (CONV_W - 1, 0), (0, 0)))
    xc = conv_b + sum(conv_w[j] * xp[:, j:j + s] for j in range(CONV_W))
    xh = xc.reshape(b, s, LRU_BLOCKS, LRU_BW)
    r = jax.nn.sigmoid(jnp.einsum('bshi,hij->bshj', xh, wa).reshape(b, s, w) + ba)
    i = jax.nn.sigmoid(jnp.einsum('bshi,hij->bshj', xh, wx).reshape(b, s, w) + bx)
    log_a = -LRU_C * r.astype(jnp.float32) * jax.nn.softplus(-lam.astype(jnp.float32))
    a = jnp.exp(log_a)
    inp = jnp.sqrt(-jnp.expm1(2.0 * log_a)) * (i * xc).astype(jnp.float32)

    def combine(lhs, rhs):
        a1, b1 = lhs
        a2, b2 = rhs
        return a1 * a2, a2 * b1 + b2

    _, h = lax.associative_scan(combine, (a, inp), axis=1)
    return h.astype(xb.dtype)


def _grouped_moe(u, router_w, router_b, w_gate, w_up, w_down):
    b, s, d = u.shape
    t = u.reshape(b * s, d)
    logits = jnp.dot(t, router_w, preferred_element_type=jnp.float32) + router_b.astype(jnp.float32)
    scores = jax.nn.softmax(logits, axis=-1)
    grouped = scores.reshape(-1, N_GROUPS, EXPERTS_PER_GROUP)
    group_score = lax.top_k(grouped, TOP_K)[0].sum(-1)
    g_sel = jnp.argmax(group_score, axis=-1)
    in_group = jnp.take_along_axis(grouped, g_sel[:, None, None], axis=1)[:, 0]
    w_top, i_top = lax.top_k(in_group, TOP_K)
    expert_idx = g_sel[:, None] * EXPERTS_PER_GROUP + i_top
    w_top = w_top / w_top.sum(-1, keepdims=True)
    combine = jnp.sum(jax.nn.one_hot(expert_idx, N_EXPERTS, dtype=jnp.float32)
                      * w_top[..., None], axis=1)
    h = jax.nn.silu(jnp.einsum('td,edf->tef', t, w_gate)) * jnp.einsum('td,edf->tef', t, w_up)
    h = h * combine[..., None].astype(h.dtype)
    out = jnp.einsum('tef,efd->td', h, w_down)
    return out.reshape(b, s, d)


def setup_inputs(seed: int = 0) -> dict:
    key = jax.random.key(seed)
    ks = jax.random.split(key, 24)

    def nrm(k, shape, scale):
        return jax.random.normal(k, shape, jnp.float32) * scale

    x = nrm(ks[0], (BATCH, SEQ, D_MODEL), 1.0)
    c = nrm(ks[1], (BATCH, D_MODEL), 1.0)
    ada_w = nrm(ks[2], (DEPTH, D_MODEL, 6 * D_MODEL), D_MODEL ** -0.5)
    ada_b = nrm(ks[3], (DEPTH, 6 * D_MODEL), 0.02)
    w_in = nrm(ks[4], (DEPTH, D_MODEL, IN_W), D_MODEL ** -0.5)
    v_lo = ATTN_W + KV_W
    w_in = w_in.at[:, :, v_lo:v_lo + KV_W].multiply(BETA)
    attn_sink = nrm(ks[5], (DEPTH, N_Q_HEADS), 0.5)
    rel_bias = nrm(ks[6], (N_BUCKETS, N_Q_HEADS), 0.5)
    conv_w = nrm(ks[7], (DEPTH, CONV_W, LRU_W), CONV_W ** -0.5)
    conv_b = nrm(ks[8], (DEPTH, LRU_W), 0.02)
    lru_wa = nrm(ks[9], (DEPTH, LRU_BLOCKS, LRU_BW, LRU_BW), LRU_BW ** -0.5)
    lru_ba = nrm(ks[10], (DEPTH, LRU_W), 0.02)
    lru_wx = nrm(ks[11], (DEPTH, LRU_BLOCKS, LRU_BW, LRU_BW), LRU_BW ** -0.5)
    lru_bx = nrm(ks[12], (DEPTH, LRU_W), 0.02)
    a_c = jax.random.uniform(ks[13], (DEPTH, LRU_W), jnp.float32, minval=0.9, maxval=0.999)
    a0 = a_c ** (1.0 / LRU_C)
    lru_lambda = jnp.log(a0) - jnp.log1p(-a0)
    w_out = nrm(ks[14], (DEPTH, D_MODEL, D_MODEL), D_MODEL ** -0.5 * BETA)
    ln1_g = 1.0 + nrm(ks[15], (DEPTH, D_MODEL), 0.02)
    ln1_b = nrm(ks[16], (DEPTH, D_MODEL), 0.02)
    router_w = nrm(ks[17], (D_MODEL, N_EXPERTS), D_MODEL ** -0.5)
    router_b = nrm(ks[18], (N_EXPERTS,), 0.01)
    moe_w_gate = nrm(ks[19], (DEPTH, N_EXPERTS, D_MODEL, EXPERT_FF), D_MODEL ** -0.5)
    moe_w_up = nrm(ks[20], (DEPTH, N_EXPERTS, D_MODEL, EXPERT_FF), D_MODEL ** -0.5)
    moe_w_down = nrm(ks[21], (DEPTH, N_EXPERTS, EXPERT_FF, D_MODEL), EXPERT_FF ** -0.5 * BETA)
    ln2_g = 1.0 + nrm(ks[22], (DEPTH, D_MODEL), 0.02)
    ln2_b = nrm(ks[23], (DEPTH, D_MODEL), 0.02)
    return {"x": x, "c": c, "ada_w": ada_w, "ada_b": ada_b, "w_in": w_in,
            "attn_sink": attn_sink, "rel_bias": rel_bias, "conv_w": conv_w, "conv_b": conv_b,
            "lru_wa": lru_wa, "lru_ba": lru_ba, "lru_wx": lru_wx, "lru_bx": lru_bx,
            "lru_lambda": lru_lambda, "w_out": w_out, "ln1_g": ln1_g, "ln1_b": ln1_b,
            "router_w": router_w, "router_b": router_b, "moe_w_gate": moe_w_gate,
            "moe_w_up": moe_w_up, "moe_w_down": moe_w_down, "ln2_g": ln2_g, "ln2_b": ln2_b}


def reference(x, c, ada_w, ada_b, w_in, attn_sink, rel_bias, conv_w, conv_b,
              lru_wa, lru_ba, lru_wx, lru_bx, lru_lambda, w_out, ln1_g, ln1_b,
              router_w, router_b, moe_w_gate, moe_w_up, moe_w_down, ln2_g, ln2_b):
    split_at = np.cumsum(IN_SPLITS)[:-1].tolist()
    c_act = jax.nn.silu(c)
    for l in range(DEPTH):
        mod = (c_act @ ada_w[l] + ada_b[l])[:, None, :]
        shift1, scale1, gate1, shift2, scale2, gate2 = jnp.split(mod, 6, axis=-1)
        u = x * (1.0 + scale1) + shift1
        q, k, v, lru_x, lru_gate, gate_a, gate_b = jnp.split(u @ w_in[l], split_at, axis=-1)
        y_a = _sliding_window_attention(q, k, v, rel_bias, attn_sink[l])
        y_b = _rg_lru(lru_x, conv_w[l], conv_b[l], lru_wa[l], lru_ba[l], lru_wx[l], lru_bx[l],
                      lru_lambda[l]) * jax.nn.gelu(lru_gate)
        y = jax.nn.sigmoid(gate_a) * y_a + jax.nn.sigmoid(gate_b) * y_b
        x = _layer_norm(ALPHA * x + gate1 * (y @ w_out[l]), ln1_g[l], ln1_b[l])
        u = x * (1.0 + scale2) + shift2
        f = _grouped_moe(u, router_w, router_b, moe_w_gate[l], moe_w_up[l], moe_w_down[l])
        x = _layer_norm(ALPHA * x + gate2 * f, ln2_g[l], ln2_b[l])
    return x
```

```python
import functools
import math

import jax
import jax.numpy as jnp
import numpy as np
from jax import lax
from jax.experimental import pallas as pl
from jax.experimental.pallas import tpu as pltpu

D_MODEL = 1024
DEPTH = 4
N_Q_HEADS = 16
N_KV_HEADS = 2
HEAD_DIM = 64
WINDOW = 128
BLOCK = 128
ATTN_W = N_Q_HEADS * HEAD_DIM
KV_W = N_KV_HEADS * HEAD_DIM
N_BUCKETS = 32
MAX_DISTANCE = 128
LRU_W = D_MODEL
LRU_BLOCKS = 8
LRU_BW = LRU_W // LRU_BLOCKS
CONV_W = 4
LRU_C = 8.0
N_EXPERTS = 16
N_GROUPS = 4
EXPERTS_PER_GROUP = N_EXPERTS // N_GROUPS
EXPERT_FF = 512
ALPHA = (2 * DEPTH) ** 0.25
LN_EPS = 1e-5

_Q0 = 0
_K0 = _Q0 + ATTN_W
_V0 = _K0 + KV_W
_LX0 = _V0 + KV_W
_LG0 = _LX0 + LRU_W
_GA0 = _LG0 + LRU_W
_GB0 = _GA0 + D_MODEL
IN_W = _GB0 + D_MODEL

N_PAIRS = N_Q_HEADS // 2
PAIRS_PER_KV = N_PAIRS // N_KV_HEADS
META_W = 128
ROW_W = D_MODEL + META_W

MIX_TS = 256
ROUTE_TM = 512
DISPATCH_TD = 512
EXPERT_TM = 256
COMBINE_TF = 512
VMEM_LIMIT = 56 * 1024 * 1024

_BF16 = jnp.bfloat16
_F32 = jnp.float32
_NEG_INF = float("-inf")


def _dot(a, b):
    return jnp.dot(a, b, preferred_element_type=_F32)


def _dot_nt(a, b, precision=None):
    return lax.dot_general(a, b, (((1,), (1,)), ((), ())), precision=precision,
                           preferred_element_type=_F32)


def _layer_norm(z, g, b):
    mu = jnp.mean(z, axis=-1, keepdims=True)
    zc = z - mu
    var = jnp.mean(zc * zc, axis=-1, keepdims=True)
    return zc * lax.rsqrt(var + LN_EPS) * g + b


def _mod_kernel(c_ref, w_ref, b_ref, o_ref):
    c = c_ref[...]
    c_act = c * jax.nn.sigmoid(c)
    o_ref[0] = _dot(c_act.astype(_BF16), w_ref[0].astype(_BF16)) + b_ref[0]


def _modulation(c_pad, ada_w, ada_b):
    depth, d, n = ada_w.shape
    tn = 1536
    rows = c_pad.shape[0]
    return pl.pallas_call(
        _mod_kernel,
        grid=(depth, n // tn),
        in_specs=[pl.BlockSpec((rows, d), lambda l, j: (0, 0)),
                  pl.BlockSpec((1, d, tn), lambda l, j: (l, 0, j)),
                  pl.BlockSpec((1, 1, tn), lambda l, j: (l, 0, j))],
        out_specs=pl.BlockSpec((1, rows, tn), lambda l, j: (l, 0, j)),
        out_shape=jax.ShapeDtypeStruct((depth, rows, n), _F32),
        name="modulation",
        compiler_params=pltpu.CompilerParams(vmem_limit_bytes=VMEM_LIMIT),
    )(c_pad, ada_w, ada_b.reshape(depth, 1, n))


def _mix_kernel(sink_ref, x_ref, mod_ref, win_ref, biasp_ref, convw_ref, convb_ref, wax_ref,
                bax_ref, lam_ref, wout_ref, lng_ref, lnb_ref, o_ref,
                q_scr, kv_scr, ya_scr, lxprev_scr, h_scr):
    ts = x_ref.shape[1]
    nblk = ts // BLOCK
    i = pl.program_id(1)

    @pl.when(i == 0)
    def _():
        kv_scr[:, 0:BLOCK, :] = jnp.zeros((8, BLOCK, KV_W), _BF16)
        lxprev_scr[...] = jnp.zeros_like(lxprev_scr)
        h_scr[...] = jnp.zeros_like(h_scr)

    x = x_ref[0]
    shift1 = mod_ref[0, 0:1, :]
    scale1 = mod_ref[0, 1:2, :]
    gate1 = mod_ref[0, 2:3, :]
    u = (x * (1.0 + scale1) + shift1).astype(_BF16)

    qkv = _dot(u, win_ref[:, _Q0:_LX0])
    q_scr[...] = (qkv[:, 0:ATTN_W] * (HEAD_DIM ** -0.5)).astype(_BF16)
    lane_kv = lax.broadcasted_iota(jnp.int32, (ts, KV_W), 1)
    low = lane_kv < HEAD_DIM
    for t, c0 in ((0, _K0), (1, _V0)):
        kv = qkv[:, c0:c0 + KV_W]
        kv_rolled = pltpu.roll(kv, HEAD_DIM, 1)
        kv_scr[4 * t + 0, BLOCK:BLOCK + ts, :] = jnp.where(low, kv, 0.0).astype(_BF16)
        kv_scr[4 * t + 1, BLOCK:BLOCK + ts, :] = jnp.where(low, 0.0, kv_rolled).astype(_BF16)
        kv_scr[4 * t + 2, BLOCK:BLOCK + ts, :] = jnp.where(low, kv_rolled, 0.0).astype(_BF16)
        kv_scr[4 * t + 3, BLOCK:BLOCK + ts, :] = jnp.where(low, 0.0, kv).astype(_BF16)

    col = lax.broadcasted_iota(jnp.int32, (BLOCK, 4 * BLOCK), 1)
    prev_cols = (col % (2 * BLOCK)) < BLOCK
    lane_o = lax.broadcasted_iota(jnp.int32, (BLOCK, 2 * HEAD_DIM), 1)
    low_o = lane_o < HEAD_DIM

    def attn_block(jb, carry):
        r0 = pl.multiple_of(jb * BLOCK, BLOCK)
        first = (i * nblk + jb) == 0
        neg = jnp.where(jnp.logical_and(prev_cols, first), _NEG_INF, 0.0)
        kb = []
        vb = []
        for g in range(N_KV_HEADS):
            kb.append(jnp.concatenate([kv_scr[2 * g, pl.ds(r0, 2 * BLOCK), :],
                                       kv_scr[2 * g + 1, pl.ds(r0, 2 * BLOCK), :]], axis=0))
            vb.append(jnp.concatenate([kv_scr[4 + 2 * g, pl.ds(r0, 2 * BLOCK), :],
                                       kv_scr[4 + 2 * g + 1, pl.ds(r0, 2 * BLOCK), :]], axis=0))
        for p in range(N_PAIRS):
            g = p // PAIRS_PER_KV
            qp = q_scr[pl.ds(r0, BLOCK), p * 128:(p + 1) * 128]
            s = _dot_nt(qp, kb[g]) + biasp_ref[p] + neg
            halves = []
            invs = []
            for hh in range(2):
                sh = s[:, hh * 2 * BLOCK:(hh + 1) * 2 * BLOCK]
                sink = sink_ref[2 * p + hh]
                m = jnp.maximum(jnp.max(sh, axis=-1, keepdims=True), sink)
                e = jnp.exp(sh - m)
                den = jnp.sum(e, axis=-1, keepdims=True) + jnp.exp(sink - m)
                halves.append(e.astype(_BF16))
                invs.append(1.0 / den)
            pv = _dot(jnp.concatenate(halves, axis=1), vb[g])
            ya_scr[pl.ds(r0, BLOCK), p * 128:(p + 1) * 128] = pv * jnp.where(low_o, invs[0], invs[1])
        return carry

    lax.fori_loop(0, nblk, attn_block, 0)

    kv_scr[:, 0:BLOCK, :] = kv_scr[:, ts:ts + BLOCK, :]

    lx = _dot(u, win_ref[:, _LX0:_LG0])
    ext = jnp.concatenate([lxprev_scr[...], lx], axis=0)
    lxprev_scr[...] = lx[ts - 8:ts, :]
    xc = convb_ref[...] + convw_ref[CONV_W - 1:CONV_W, :] * lx
    for j in range(1, CONV_W):
        shifted = pltpu.roll(ext, j, 0)[8:8 + ts, :]
        xc = xc + convw_ref[CONV_W - 1 - j:CONV_W - j, :] * shifted
    xcb = xc.astype(_BF16)
    r_parts = []
    i_parts = []
    for hb in range(LRU_BLOCKS):
        ri = _dot(xcb[:, hb * LRU_BW:(hb + 1) * LRU_BW], wax_ref[hb])
        r_parts.append(ri[:, 0:LRU_BW])
        i_parts.append(ri[:, LRU_BW:2 * LRU_BW])
    r = jax.nn.sigmoid(jnp.concatenate(r_parts, axis=1) + bax_ref[0:1, :])
    ig = jax.nn.sigmoid(jnp.concatenate(i_parts, axis=1) + bax_ref[1:2, :])
    nlam = -lam_ref[...]
    softplus = jnp.maximum(nlam, 0.0) + jnp.log(1.0 + jnp.exp(-jnp.abs(nlam)))
    log_a = (-LRU_C) * r * softplus
    a = jnp.exp(log_a)
    bv = jnp.sqrt(1.0 - a * a) * (ig * xc)

    row8 = lax.broadcasted_iota(jnp.int32, (ts, LRU_W), 0) % 8
    for d in (1, 2, 4):
        a_s = pltpu.roll(a, d, 0)
        b_s = pltpu.roll(bv, d, 0)
        inside = row8 >= d
        bv = jnp.where(inside, a * b_s + bv, bv)
        a = jnp.where(inside, a * a_s, a)
    h = h_scr[...]
    h_parts = []
    for g8 in range(ts // 8):
        hg = bv[8 * g8:8 * g8 + 8, :] + a[8 * g8:8 * g8 + 8, :] * h
        h_parts.append(hg)
        h = hg[7:8, :]
    h_scr[...] = h
    hseq = jnp.concatenate(h_parts, axis=0)

    yb = hseq * jax.nn.gelu(_dot(u, win_ref[:, _LG0:_GA0]))

    ga = jax.nn.sigmoid(_dot(u, win_ref[:, _GA0:_GB0]))
    gb = jax.nn.sigmoid(_dot(u, win_ref[:, _GB0:IN_W]))
    y = (ga * ya_scr[...] + gb * yb).astype(_BF16)
    z = ALPHA * x + gate1 * _dot(y, wout_ref[...])
    o_ref[0] = _layer_norm(z, lng_ref[...], lnb_ref[...])


def _mix(x, mod, sink, win, biasp, convw, convb, wax, bax, lam, wout, lng, lnb):
    b, s, d = x.shape
    ts = MIX_TS
    const2 = lambda bi, i, sk: (0, 0)
    const3 = lambda bi, i, sk: (0, 0, 0)
    grid_spec = pltpu.PrefetchScalarGridSpec(
        num_scalar_prefetch=1,
        grid=(b, s // ts),
        in_specs=[
            pl.BlockSpec((1, ts, d), lambda bi, i, sk: (bi, i, 0)),
            pl.BlockSpec((1, 6, d), lambda bi, i, sk: (bi, 0, 0)),
            pl.BlockSpec((d, IN_W), const2),
            pl.BlockSpec((N_PAIRS, BLOCK, 4 * BLOCK), const3),
            pl.BlockSpec((CONV_W, d), const2),
            pl.BlockSpec((1, d), const2),
            pl.BlockSpec((LRU_BLOCKS, LRU_BW, 2 * LRU_BW), const3),
            pl.BlockSpec((2, d), const2),
            pl.BlockSpec((1, d), const2),
            pl.BlockSpec((d, d), const2),
            pl.BlockSpec((1, d), const2),
            pl.BlockSpec((1, d), const2),
        ],
        out_specs=pl.BlockSpec((1, ts, d), lambda bi, i, sk: (bi, i, 0)),
        scratch_shapes=[
            pltpu.VMEM((ts, ATTN_W), _BF16),
            pltpu.VMEM((8, ts + BLOCK, KV_W), _BF16),
            pltpu.VMEM((ts, ATTN_W), _F32),
            pltpu.VMEM((8, LRU_W), _F32),
            pltpu.VMEM((1, LRU_W), _F32),
        ],
    )
    return pl.pallas_call(
        _mix_kernel,
        grid_spec=grid_spec,
        out_shape=jax.ShapeDtypeStruct((b, s, d), _F32),
        name="mix",
        compiler_params=pltpu.CompilerParams(
            dimension_semantics=("arbitrary", "arbitrary"), vmem_limit_bytes=VMEM_LIMIT),
    )(sink, x, mod, win, biasp, convw, convb, wax, bax, lam, wout, lng, lnb)


def _second_largest(v0, v1, v2, v3):
    hi1, lo1 = jnp.maximum(v0, v1), jnp.minimum(v0, v1)
    hi2, lo2 = jnp.maximum(v2, v3), jnp.minimum(v2, v3)
    return jnp.maximum(hi1, hi2), jnp.maximum(jnp.minimum(hi1, hi2), jnp.maximum(lo1, lo2))


def _route_kernel(x_ref, mod_ref, rwt_ref, rb_ref, tri_ref, row_ref, pos_ref, cnt_ref, cnt_scr,
                  *, n_tokens):
    tm = x_ref.shape[0]
    j = pl.program_id(0)

    @pl.when(j == 0)
    def _():
        cnt_scr[...] = jnp.zeros_like(cnt_scr)

    shift2 = mod_ref[0, 3:4, :]
    scale2 = mod_ref[0, 4:5, :]
    u = x_ref[...] * (1.0 + scale2) + shift2
    row_ref[:, 0:D_MODEL] = u

    logits = _dot_nt(rwt_ref[...], u, precision=lax.Precision.HIGHEST) + rb_ref[...]
    mx = jnp.max(logits, axis=0, keepdims=True)
    ex = jnp.exp(logits - mx)
    scores = ex / jnp.sum(ex, axis=0, keepdims=True)
    sc = [scores[e:e + 1, :] for e in range(N_EXPERTS)]

    gscore = []
    for g in range(N_GROUPS):
        top1, top2 = _second_largest(*sc[4 * g:4 * g + 4])
        gscore.append(top1 + top2)
    sel = []
    taken = None
    for g in range(N_GROUPS):
        best = None
        for g2 in range(g + 1, N_GROUPS):
            c = gscore[g] >= gscore[g2]
            best = c if best is None else jnp.logical_and(best, c)
        if best is None:
            best = jnp.ones_like(gscore[g], dtype=jnp.bool_)
        if taken is not None:
            best = jnp.logical_and(best, jnp.logical_not(taken))
        sel.append(best)
        taken = best if taken is None else jnp.logical_or(taken, best)

    vals = []
    for k in range(EXPERTS_PER_GROUP):
        v = sc[12 + k]
        for g in (2, 1, 0):
            v = jnp.where(sel[g], sc[4 * g + k], v)
        vals.append(v)
    ranks = []
    for k in range(EXPERTS_PER_GROUP):
        rk = jnp.zeros_like(vals[k])
        for k2 in range(EXPERTS_PER_GROUP):
            if k2 == k:
                continue
            beats = (vals[k2] >= vals[k]) if k2 < k else (vals[k2] > vals[k])
            rk = rk + jnp.where(beats, 1.0, 0.0)
        ranks.append(rk)
    top_a = jnp.zeros_like(vals[0])
    top_b = jnp.zeros_like(vals[0])
    for k in range(EXPERTS_PER_GROUP):
        top_a = jnp.where(ranks[k] == 0.0, vals[k], top_a)
        top_b = jnp.where(ranks[k] == 1.0, vals[k], top_b)
    denom = top_a + top_b
    cw = [jnp.where(ranks[k] < 2.0, vals[k] / denom, 0.0) for k in range(EXPERTS_PER_GROUP)]

    sub = lax.broadcasted_iota(jnp.int32, (8, tm), 0)
    cw8 = jnp.zeros((8, tm), _F32)
    for k in range(EXPERTS_PER_GROUP):
        cw8 = jnp.where(sub == k, cw[k], cw8)
    meta_t = jnp.concatenate([cw8, jnp.zeros((META_W - 8, tm), _F32)], axis=0)
    row_ref[:, D_MODEL:ROW_W] = meta_t.T

    onehot = jnp.zeros((8, tm), _F32)
    for g in range(N_GROUPS):
        onehot = jnp.where(jnp.logical_and(sub == g, sel[g]), 1.0, onehot)
    before = _dot(onehot.astype(_BF16), tri_ref[...])
    cnt = cnt_scr[...]
    rank = jnp.sum(onehot * (before + cnt[:, 0:1]), axis=0, keepdims=True)
    gidx = jnp.sum(onehot * sub.astype(_F32), axis=0, keepdims=True)
    pos_ref[0] = (gidx * float(n_tokens) + rank).astype(jnp.int32)
    cnt = cnt + jnp.sum(onehot, axis=1, keepdims=True)
    cnt_scr[...] = cnt
    cnt_ref[...] = cnt.astype(jnp.int32)


def _route(x1, mod, rwt, rb, tri, tokens_per_batch):
    t, d = x1.shape
    tm = ROUTE_TM
    per_b = tokens_per_batch // tm
    return pl.pallas_call(
        functools.partial(_route_kernel, n_tokens=t),
        grid=(t // tm,),
        in_specs=[pl.BlockSpec((tm, d), lambda j: (j, 0)),
                  pl.BlockSpec((1, 6, d), lambda j: (j // per_b, 0, 0)),
                  pl.BlockSpec((N_EXPERTS, d), lambda j: (0, 0)),
                  pl.BlockSpec((N_EXPERTS, 1), lambda j: (0, 0)),
                  pl.BlockSpec((tm, tm), lambda j: (0, 0))],
        out_specs=[pl.BlockSpec((tm, ROW_W), lambda j: (j, 0)),
                   pl.BlockSpec((1, 1, tm), lambda j: (j, 0, 0)),
                   pl.BlockSpec((8, 128), lambda j: (0, 0))],
        out_shape=[jax.ShapeDtypeStruct((t, ROW_W), _F32),
                   jax.ShapeDtypeStruct((t // tm, 1, tm), jnp.int32),
                   jax.ShapeDtypeStruct((8, 128), jnp.int32)],
        scratch_shapes=[pltpu.VMEM((8, 128), _F32)],
        name="route",
        compiler_params=pltpu.CompilerParams(
            dimension_semantics=("arbitrary",), vmem_limit_bytes=VMEM_LIMIT),
    )(x1, mod, rwt, rb, tri)


def _dispatch_kernel(pos_ref, cnt_ref, rows_ref, xs_ref, zero_scr, sem, *, n_tokens):
    td = rows_ref.shape[0]
    j = pl.program_id(0)
    base = j * td

    def row_copy(r, slot):
        return pltpu.make_async_copy(rows_ref.at[pl.ds(r, 1), :], xs_ref.at[pl.ds(slot, 1), :], sem)

    def start(r, carry):
        row_copy(r, pos_ref[base + r]).start()
        return carry

    def wait(r, carry):
        row_copy(0, 0).wait()
        return carry

    lax.fori_loop(0, td, start, 0, unroll=8)
    lax.fori_loop(0, td, wait, 0, unroll=8)

    @pl.when(j == pl.num_programs(0) - 1)
    def _():
        zero_scr[...] = jnp.zeros_like(zero_scr)

        def pad_copy(slot):
            return pltpu.make_async_copy(zero_scr.at[pl.ds(0, 1), :], xs_ref.at[pl.ds(slot, 1), :], sem)

        for g in range(N_GROUPS):
            c = cnt_ref[g]
            n_pad = (EXPERT_TM - c % EXPERT_TM) % EXPERT_TM

            def pstart(k, carry, c=c, g=g):
                pad_copy(g * n_tokens + c + k).start()
                return carry

            def pwait(k, carry):
                pad_copy(0).wait()
                return carry

            lax.fori_loop(0, n_pad, pstart, 0)
            lax.fori_loop(0, n_pad, pwait, 0)


def _dispatch(pos, cnt, rows):
    t = rows.shape[0]
    td = DISPATCH_TD
    grid_spec = pltpu.PrefetchScalarGridSpec(
        num_scalar_prefetch=2,
        grid=(t // td,),
        in_specs=[pl.BlockSpec((td, ROW_W), lambda j, p, c: (j, 0))],
        out_specs=pl.BlockSpec(memory_space=pl.ANY),
        scratch_shapes=[pltpu.VMEM((8, ROW_W), _F32), pltpu.SemaphoreType.DMA(())],
    )
    return pl.pallas_call(
        functools.partial(_dispatch_kernel, n_tokens=t),
        grid_spec=grid_spec,
        out_shape=jax.ShapeDtypeStruct((N_GROUPS * t, ROW_W), _F32),
        name="dispatch",
        compiler_params=pltpu.CompilerParams(
            dimension_semantics=("arbitrary",), vmem_limit_bytes=VMEM_LIMIT),
    )(pos, cnt, rows)


def _expert_kernel(blk_ref, grp_ref, live_ref, xs_ref, wg_ref, wu_ref, wd_ref, ys_ref):
    j = pl.program_id(0)

    @pl.when(live_ref[j] != 0)
    def _():
        x = xs_ref[:, 0:D_MODEL].astype(_BF16)
        acc = None
        for e in range(EXPERTS_PER_GROUP):
            gate = _dot(x, wg_ref[e])
            up = _dot(x, wu_ref[e])
            h = gate * jax.nn.sigmoid(gate) * up * xs_ref[:, D_MODEL + e:D_MODEL + e + 1]
            part = _dot(h.astype(_BF16), wd_ref[e])
            acc = part if acc is None else acc + part
        ys_ref[...] = acc


def _experts(blk, grp, live, xs, wg, wu, wd):
    tm = EXPERT_TM
    n_tiles = blk.shape[0]
    grid_spec = pltpu.PrefetchScalarGridSpec(
        num_scalar_prefetch=3,
        grid=(n_tiles,),
        in_specs=[
            pl.BlockSpec((tm, ROW_W), lambda j, b, g, v: (b[j], 0)),
            pl.BlockSpec((EXPERTS_PER_GROUP, D_MODEL, EXPERT_FF), lambda j, b, g, v: (g[j], 0, 0)),
            pl.BlockSpec((EXPERTS_PER_GROUP, D_MODEL, EXPERT_FF), lambda j, b, g, v: (g[j], 0, 0)),
            pl.BlockSpec((EXPERTS_PER_GROUP, EXPERT_FF, D_MODEL), lambda j, b, g, v: (g[j], 0, 0)),
        ],
        out_specs=pl.BlockSpec((tm, D_MODEL), lambda j, b, g, v: (b[j], 0)),
    )
    return pl.pallas_call(
        _expert_kernel,
        grid_spec=grid_spec,
        out_shape=jax.ShapeDtypeStruct((xs.shape[0], D_MODEL), _F32),
        name="experts",
        compiler_params=pltpu.CompilerParams(
            dimension_semantics=("arbitrary",), vmem_limit_bytes=VMEM_LIMIT),
    )(blk, grp, live, xs, wg, wu, wd)


def _tile_plan(cnt, n_tokens):
    tm = EXPERT_TM
    n_tiles = n_tokens // tm + N_GROUPS
    per_group = (cnt + tm - 1) // tm
    ends = jnp.cumsum(per_group)
    starts = ends - per_group
    j = jnp.arange(n_tiles, dtype=jnp.int32)
    total = ends[-1]
    jc = jnp.minimum(j, total - 1)
    grp = jnp.sum((jc[:, None] >= ends[None, :]).astype(jnp.int32), axis=1)
    blk = grp * (n_tokens // tm) + (jc - starts[grp])
    live = (j < total).astype(jnp.int32)
    return blk.astype(jnp.int32), grp.astype(jnp.int32), live


def _combine_kernel(pos_ref, x_ref, mod_ref, lng_ref, lnb_ref, ys_ref, o_ref, f_scr, sem):
    tf = x_ref.shape[0]
    j = pl.program_id(0)
    base = j * tf

    def row_copy(r, slot):
        return pltpu.make_async_copy(ys_ref.at[pl.ds(slot, 1), :], f_scr.at[pl.ds(r, 1), :], sem)

    def start(r, carry):
        row_copy(r, pos_ref[base + r]).start()
        return carry

    def wait(r, carry):
        row_copy(0, 0).wait()
        return carry

    lax.fori_loop(0, tf, start, 0, unroll=8)
    lax.fori_loop(0, tf, wait, 0, unroll=8)
    gate2 = mod_ref[0, 5:6, :]
    z = ALPHA * x_ref[...] + gate2 * f_scr[...]
    o_ref[...] = _layer_norm(z, lng_ref[...], lnb_ref[...])


def _combine(pos, x1, mod, lng, lnb, ys, tokens_per_batch):
    t, d = x1.shape
    tf = COMBINE_TF
    per_b = tokens_per_batch // tf
    grid_spec = pltpu.PrefetchScalarGridSpec(
        num_scalar_prefetch=1,
        grid=(t // tf,),
        in_specs=[pl.BlockSpec((tf, d), lambda j, p: (j, 0)),
                  pl.BlockSpec((1, 6, d), lambda j, p: (j // per_b, 0, 0)),
                  pl.BlockSpec((1, d), lambda j, p: (0, 0)),
                  pl.BlockSpec((1, d), lambda j, p: (0, 0)),
                  pl.BlockSpec(memory_space=pl.ANY)],
        out_specs=pl.BlockSpec((tf, d), lambda j, p: (j, 0)),
        scratch_shapes=[pltpu.VMEM((tf, d), _F32), pltpu.SemaphoreType.DMA(())],
    )
    return pl.pallas_call(
        _combine_kernel,
        grid_spec=grid_spec,
        out_shape=jax.ShapeDtypeStruct((t, d), _F32),
        name="combine",
        compiler_params=pltpu.CompilerParams(
            dimension_semantics=("arbitrary",), vmem_limit_bytes=VMEM_LIMIT),
    )(pos, x1, mod, lng, lnb, ys)


def _t5_causal_bucket(dist):
    max_exact = N_BUCKETS // 2
    d = np.maximum(dist, 0)
    df = np.maximum(d, 1).astype(np.float32)
    large = max_exact + (np.log(df / max_exact) / math.log(MAX_DISTANCE / max_exact)
                         * (N_BUCKETS - max_exact)).astype(np.int32)
    large = np.minimum(large, N_BUCKETS - 1)
    return np.where(d < max_exact, d, large).astype(np.int32)


def _pair_bias(rel_bias):
    qi = np.arange(BLOCK)[:, None]
    sj = np.arange(2 * BLOCK)[None, :]
    dist = qi + BLOCK - sj
    in_window = (dist >= 0) & (dist < WINDOW)
    bias = rel_bias.astype(_F32)[_t5_causal_bucket(dist)]
    bias = jnp.where(in_window[:, :, None], bias, _NEG_INF)
    bias = jnp.transpose(bias, (2, 0, 1)).reshape(N_PAIRS, 2, BLOCK, 2 * BLOCK)
    return jnp.transpose(bias, (0, 2, 1, 3)).reshape(N_PAIRS, BLOCK, 4 * BLOCK)


def kernel(x, c, ada_w, ada_b, w_in, attn_sink, rel_bias, conv_w, conv_b, lru_wa, lru_ba, lru_wx,
           lru_bx, lru_lambda, w_out, ln1_g, ln1_b, router_w, router_b, moe_w_gate, moe_w_up,
           moe_w_down, ln2_g, ln2_b):
    b, s, d = x.shape
    t = b * s
    depth = w_in.shape[0]

    c_pad = jnp.pad(c, ((0, 8 - b), (0, 0)))
    mod_all = _modulation(c_pad, ada_w, ada_b)
    mod_all = mod_all[:, :b, :].reshape(depth, b, 6, d)

    biasp = _pair_bias(rel_bias)
    rwt = router_w.T
    rb = router_b.reshape(N_EXPERTS, 1)
    tri = jnp.asarray(np.triu(np.ones((ROUTE_TM, ROUTE_TM), np.float32), 1), _BF16)

    for l in range(depth):
        mod = mod_all[l]
        wax = jnp.concatenate([lru_wa[l], lru_wx[l]], axis=-1).astype(_BF16)
        bax = jnp.stack([lru_ba[l], lru_bx[l]], axis=0)
        x1 = _mix(x, mod, attn_sink[l], w_in[l].astype(_BF16), biasp, conv_w[l],
                  conv_b[l].reshape(1, d), wax, bax, lru_lambda[l].reshape(1, d),
                  w_out[l].astype(_BF16), ln1_g[l].reshape(1, d), ln1_b[l].reshape(1, d))
        x1 = x1.reshape(t, d)
        rows, pos, cnt = _route(x1, mod, rwt, rb, tri, s)
        pos = pos.reshape(t)
        cnt = cnt[:N_GROUPS, 0]
        xs = _dispatch(pos, cnt, rows)
        blk, grp, live = _tile_plan(cnt, t)
        ys = _experts(blk, grp, live, xs, moe_w_gate[l].astype(_BF16), moe_w_up[l].astype(_BF16),
                      moe_w_down[l].astype(_BF16))
        x = _combine(pos, x1, mod, ln2_g[l].reshape(1, d), ln2_b[l].reshape(1, d), ys, s)
        x = x.reshape(b, s, d)
    return x
```

```python
import functools
import math

import jax
import jax.numpy as jnp
import numpy as np
from jax import lax
from jax.experimental import pallas as pl
from jax.experimental.pallas import tpu as pltpu

D_MODEL = 1024
DEPTH = 4
N_Q_HEADS = 16
N_KV_HEADS = 2
HEAD_DIM = 64
WINDOW = 128
BLOCK = 128
ATTN_W = N_Q_HEADS * HEAD_DIM
KV_W = N_KV_HEADS * HEAD_DIM
N_BUCKETS = 32
MAX_DISTANCE = 128
LRU_W = D_MODEL
LRU_BLOCKS = 8
LRU_BW = LRU_W // LRU_BLOCKS
CONV_W = 4
LRU_C = 8.0
N_EXPERTS = 16
N_GROUPS = 4
EXPERTS_PER_GROUP = N_EXPERTS // N_GROUPS
EXPERT_FF = 512
ALPHA = (2 * DEPTH) ** 0.25
LN_EPS = 1e-5

_Q0 = 0
_K0 = _Q0 + ATTN_W
_V0 = _K0 + KV_W
_LX0 = _V0 + KV_W
_LG0 = _LX0 + LRU_W
_GA0 = _LG0 + LRU_W
_GB0 = _GA0 + D_MODEL
IN_W = _GB0 + D_MODEL

N_PAIRS = N_Q_HEADS // 2
PAIRS_PER_KV = N_PAIRS // N_KV_HEADS
META_W = 128
ROW_W = D_MODEL + META_W

MIX_TS = 256
ROUTE_TM = 512
DISPATCH_TD = 512
EXPERT_TM = 256
COMBINE_TF = 512
VMEM_LIMIT = 56 * 1024 * 1024

_BF16 = jnp.bfloat16
_F32 = jnp.float32
_NEG_INF = float("-inf")


def _dot(a, b):
    return jnp.dot(a, b, preferred_element_type=_F32)


def _dot_nt(a, b, precision=None):
    return lax.dot_general(a, b, (((1,), (1,)), ((), ())), precision=precision,
                           preferred_element_type=_F32)


def _layer_norm(z, g, b):
    mu = jnp.mean(z, axis=-1, keepdims=True)
    zc = z - mu
    var = jnp.mean(zc * zc, axis=-1, keepdims=True)
    return zc * lax.rsqrt(var + LN_EPS) * g + b


def _mod_kernel(c_ref, w_ref, b_ref, o_ref):
    c = c_ref[...]
    c_act = c * jax.nn.sigmoid(c)
    o_ref[0] = _dot(c_act.astype(_BF16), w_ref[0].astype(_BF16)) + b_ref[0]


def _modulation(c_pad, ada_w, ada_b):
    depth, d, n = ada_w.shape
    tn = 1536
    rows = c_pad.shape[0]
    return pl.pallas_call(
        _mod_kernel,
        grid=(depth, n // tn),
        in_specs=[pl.BlockSpec((rows, d), lambda l, j: (0, 0)),
                  pl.BlockSpec((1, d, tn), lambda l, j: (l, 0, j)),
                  pl.BlockSpec((1, 1, tn), lambda l, j: (l, 0, j))],
        out_specs=pl.BlockSpec((1, rows, tn), lambda l, j: (l, 0, j)),
        out_shape=jax.ShapeDtypeStruct((depth, rows, n), _F32),
        name="modulation",
        compiler_params=pltpu.CompilerParams(vmem_limit_bytes=VMEM_LIMIT),
    )(c_pad, ada_w, ada_b.reshape(depth, 1, n))


def _mix_kernel(sink_ref, x_ref, mod_ref, win_ref, bias_ref, convw_ref, convb_ref, wax_ref,
                bax_ref, lam_ref, wout_ref, lng_ref, lnb_ref, o_ref,
                q_scr, k_scr, v_scr, ya_scr, lxprev_scr, h_scr):
    ts = x_ref.shape[1]
    nblk = ts // BLOCK
    i = pl.program_id(1)

    @pl.when(i == 0)
    def _():
        k_scr[:, 0:BLOCK, :] = jnp.zeros((4, BLOCK, KV_W), _BF16)
        v_scr[:, 0:BLOCK, :] = jnp.zeros((4, BLOCK, 2 * KV_W), _BF16)
        lxprev_scr[...] = jnp.zeros_like(lxprev_scr)
        h_scr[...] = jnp.zeros_like(h_scr)

    x = x_ref[0]
    shift1 = mod_ref[0, 0:1, :]
    scale1 = mod_ref[0, 1:2, :]
    gate1 = mod_ref[0, 2:3, :]
    u = (x * (1.0 + scale1) + shift1).astype(_BF16)

    qkv = _dot(u, win_ref[:, _Q0:_LX0])
    q_scr[...] = (qkv[:, 0:ATTN_W] * (HEAD_DIM ** -0.5)).astype(_BF16)
    low = lax.broadcasted_iota(jnp.int32, (ts, KV_W), 1) < HEAD_DIM
    k_new = qkv[:, _K0:_K0 + KV_W]
    k_rolled = pltpu.roll(k_new, HEAD_DIM, 1)
    v_new = qkv[:, _V0:_V0 + KV_W]
    v_rolled = pltpu.roll(v_new, HEAD_DIM, 1)
    k_scr[0, BLOCK:BLOCK + ts, :] = jnp.where(low, k_new, 0.0).astype(_BF16)
    k_scr[1, BLOCK:BLOCK + ts, :] = jnp.where(low, 0.0, k_rolled).astype(_BF16)
    k_scr[2, BLOCK:BLOCK + ts, :] = jnp.where(low, k_rolled, 0.0).astype(_BF16)
    k_scr[3, BLOCK:BLOCK + ts, :] = jnp.where(low, 0.0, k_new).astype(_BF16)
    ones_lo = jnp.where(low, 1.0, 0.0).astype(_BF16)
    ones_hi = jnp.where(low, 0.0, 1.0).astype(_BF16)
    v_scr[0, BLOCK:BLOCK + ts, 0:KV_W] = jnp.where(low, v_new, 0.0).astype(_BF16)
    v_scr[1, BLOCK:BLOCK + ts, 0:KV_W] = jnp.where(low, 0.0, v_rolled).astype(_BF16)
    v_scr[2, BLOCK:BLOCK + ts, 0:KV_W] = jnp.where(low, v_rolled, 0.0).astype(_BF16)
    v_scr[3, BLOCK:BLOCK + ts, 0:KV_W] = jnp.where(low, 0.0, v_new).astype(_BF16)
    for slot in range(4):
        v_scr[slot, BLOCK:BLOCK + ts, KV_W:2 * KV_W] = ones_lo if slot % 2 == 0 else ones_hi

    prev_cols = lax.broadcasted_iota(jnp.int32, (BLOCK, 2 * BLOCK), 1) < BLOCK
    low_o = lax.broadcasted_iota(jnp.int32, (BLOCK, 2 * HEAD_DIM), 1) < HEAD_DIM

    def attn_block(jb, carry):
        r0 = pl.multiple_of(jb * BLOCK, BLOCK)
        first = (i * nblk + jb) == 0
        neg = jnp.where(jnp.logical_and(prev_cols, first), _NEG_INF, 0.0)
        for p in range(N_PAIRS):
            g = p // PAIRS_PER_KV
            qp = q_scr[pl.ds(r0, BLOCK), p * 128:(p + 1) * 128]
            acc = None
            sink_terms = []
            for hh in range(2):
                s = (_dot_nt(qp, k_scr[2 * g + hh, pl.ds(r0, 2 * BLOCK), :])
                     + bias_ref[2 * p + hh] + neg)
                sink = sink_ref[2 * p + hh]
                m = jnp.maximum(jnp.max(s, axis=-1, keepdims=True), sink)
                e = jnp.exp(s - m).astype(_BF16)
                sink_terms.append(jnp.exp(sink - m))
                part = _dot(e, v_scr[2 * g + hh, pl.ds(r0, 2 * BLOCK), :])
                acc = part if acc is None else acc + part
            den = acc[:, 2 * HEAD_DIM:] + jnp.where(low_o, sink_terms[0], sink_terms[1])
            ya_scr[pl.ds(r0, BLOCK), p * 128:(p + 1) * 128] = acc[:, 0:2 * HEAD_DIM] / den
        return carry

    lax.fori_loop(0, nblk, attn_block, 0)

    k_scr[:, 0:BLOCK, :] = k_scr[:, ts:ts + BLOCK, :]
    v_scr[:, 0:BLOCK, :] = v_scr[:, ts:ts + BLOCK, :]

    lx = _dot(u, win_ref[:, _LX0:_LG0])
    ext = jnp.concatenate([lxprev_scr[...], lx], axis=0)
    lxprev_scr[...] = lx[ts - 8:ts, :]
    xc = convb_ref[...] + convw_ref[CONV_W - 1:CONV_W, :] * lx
    for j in range(1, CONV_W):
        shifted = pltpu.roll(ext, j, 0)[8:8 + ts, :]
        xc = xc + convw_ref[CONV_W - 1 - j:CONV_W - j, :] * shifted
    xcb = xc.astype(_BF16)
    r_parts = []
    i_parts = []
    for hb in range(LRU_BLOCKS):
        ri = _dot(xcb[:, hb * LRU_BW:(hb + 1) * LRU_BW], wax_ref[hb])
        r_parts.append(ri[:, 0:LRU_BW])
        i_parts.append(ri[:, LRU_BW:2 * LRU_BW])
    r = jax.nn.sigmoid(jnp.concatenate(r_parts, axis=1) + bax_ref[0:1, :])
    ig = jax.nn.sigmoid(jnp.concatenate(i_parts, axis=1) + bax_ref[1:2, :])
    nlam = -lam_ref[...]
    softplus = jnp.maximum(nlam, 0.0) + jnp.log(1.0 + jnp.exp(-jnp.abs(nlam)))
    a = jnp.exp((-LRU_C) * r * softplus)
    om = 1.0 - a * a
    root = jnp.where(om > 0.0, om * lax.rsqrt(om), 0.0)
    bv = root * (ig * xc)

    row8 = lax.broadcasted_iota(jnp.int32, (ts, LRU_W), 0) % 8
    for d in (1, 2, 4):
        a_s = pltpu.roll(a, d, 0)
        b_s = pltpu.roll(bv, d, 0)
        inside = row8 >= d
        bv = jnp.where(inside, a * b_s + bv, bv)
        a = jnp.where(inside, a * a_s, a)
    h = h_scr[...]
    h_parts = []
    for g8 in range(ts // 8):
        hg = bv[8 * g8:8 * g8 + 8, :] + a[8 * g8:8 * g8 + 8, :] * h
        h_parts.append(hg)
        h = hg[7:8, :]
    h_scr[...] = h
    hseq = jnp.concatenate(h_parts, axis=0)

    yb = hseq * jax.nn.gelu(_dot(u, win_ref[:, _LG0:_GA0]))

    ga = jax.nn.sigmoid(_dot(u, win_ref[:, _GA0:_GB0]))
    gb = jax.nn.sigmoid(_dot(u, win_ref[:, _GB0:IN_W]))
    y = (ga * ya_scr[...] + gb * yb).astype(_BF16)
    z = ALPHA * x + gate1 * _dot(y, wout_ref[...])
    o_ref[0] = _layer_norm(z, lng_ref[...], lnb_ref[...])


def _mix(x, mod, sink, win, biasp, convw, convb, wax, bax, lam, wout, lng, lnb):
    b, s, d = x.shape
    ts = MIX_TS
    const2 = lambda bi, i, sk: (0, 0)
    const3 = lambda bi, i, sk: (0, 0, 0)
    grid_spec = pltpu.PrefetchScalarGridSpec(
        num_scalar_prefetch=1,
        grid=(b, s // ts),
        in_specs=[
            pl.BlockSpec((1, ts, d), lambda bi, i, sk: (bi, i, 0)),
            pl.BlockSpec((1, 6, d), lambda bi, i, sk: (bi, 0, 0)),
            pl.BlockSpec((d, IN_W), const2),
            pl.BlockSpec((N_Q_HEADS, BLOCK, 2 * BLOCK), const3),
            pl.BlockSpec((CONV_W, d), const2),
            pl.BlockSpec((1, d), const2),
            pl.BlockSpec((LRU_BLOCKS, LRU_BW, 2 * LRU_BW), const3),
            pl.BlockSpec((2, d), const2),
            pl.BlockSpec((1, d), const2),
            pl.BlockSpec((d, d), const2),
            pl.BlockSpec((1, d), const2),
            pl.BlockSpec((1, d), const2),
        ],
        out_specs=pl.BlockSpec((1, ts, d), lambda bi, i, sk: (bi, i, 0)),
        scratch_shapes=[
            pltpu.VMEM((ts, ATTN_W), _BF16),
            pltpu.VMEM((4, ts + BLOCK, KV_W), _BF16),
            pltpu.VMEM((4, ts + BLOCK, 2 * KV_W), _BF16),
            pltpu.VMEM((ts, ATTN_W), _F32),
            pltpu.VMEM((8, LRU_W), _F32),
            pltpu.VMEM((1, LRU_W), _F32),
        ],
    )
    return pl.pallas_call(
        _mix_kernel,
        grid_spec=grid_spec,
        out_shape=jax.ShapeDtypeStruct((b, s, d), _F32),
        name="mix",
        compiler_params=pltpu.CompilerParams(
            dimension_semantics=("arbitrary", "arbitrary"), vmem_limit_bytes=VMEM_LIMIT),
    )(sink, x, mod, win, biasp, convw, convb, wax, bax, lam, wout, lng, lnb)


def _second_largest(v0, v1, v2, v3):
    hi1, lo1 = jnp.maximum(v0, v1), jnp.minimum(v0, v1)
    hi2, lo2 = jnp.maximum(v2, v3), jnp.minimum(v2, v3)
    return jnp.maximum(hi1, hi2), jnp.maximum(jnp.minimum(hi1, hi2), jnp.maximum(lo1, lo2))


def _route_kernel(x_ref, mod_ref, rwt_ref, rb_ref, tri_ref, row_ref, pos_ref, cnt_ref, cnt_scr,
                  *, n_tokens):
    tm = x_ref.shape[0]
    j = pl.program_id(0)

    @pl.when(j == 0)
    def _():
        cnt_scr[...] = jnp.zeros_like(cnt_scr)

    shift2 = mod_ref[0, 3:4, :]
    scale2 = mod_ref[0, 4:5, :]
    u = x_ref[...] * (1.0 + scale2) + shift2
    row_ref[:, 0:D_MODEL] = u

    logits = _dot_nt(rwt_ref[...], u, precision=lax.Precision.HIGHEST) + rb_ref[...]
    mx = jnp.max(logits, axis=0, keepdims=True)
    ex = jnp.exp(logits - mx)
    scores = ex / jnp.sum(ex, axis=0, keepdims=True)
    sc = [scores[e:e + 1, :] for e in range(N_EXPERTS)]

    gscore = []
    for g in range(N_GROUPS):
        top1, top2 = _second_largest(*sc[4 * g:4 * g + 4])
        gscore.append(top1 + top2)
    sel = []
    taken = None
    for g in range(N_GROUPS):
        best = None
        for g2 in range(g + 1, N_GROUPS):
            c = gscore[g] >= gscore[g2]
            best = c if best is None else jnp.logical_and(best, c)
        if best is None:
            best = jnp.ones_like(gscore[g], dtype=jnp.bool_)
        if taken is not None:
            best = jnp.logical_and(best, jnp.logical_not(taken))
        sel.append(best)
        taken = best if taken is None else jnp.logical_or(taken, best)

    vals = []
    for k in range(EXPERTS_PER_GROUP):
        v = sc[12 + k]
        for g in (2, 1, 0):
            v = jnp.where(sel[g], sc[4 * g + k], v)
        vals.append(v)
    ranks = []
    for k in range(EXPERTS_PER_GROUP):
        rk = jnp.zeros_like(vals[k])
        for k2 in range(EXPERTS_PER_GROUP):
            if k2 == k:
                continue
            beats = (vals[k2] >= vals[k]) if k2 < k else (vals[k2] > vals[k])
            rk = rk + jnp.where(beats, 1.0, 0.0)
        ranks.append(rk)
    top_a = jnp.zeros_like(vals[0])
    top_b = jnp.zeros_like(vals[0])
    for k in range(EXPERTS_PER_GROUP):
        top_a = jnp.where(ranks[k] == 0.0, vals[k], top_a)
        top_b = jnp.where(ranks[k] == 1.0, vals[k], top_b)
    denom = top_a + top_b
    cw = [jnp.where(ranks[k] < 2.0, vals[k] / denom, 0.0) for k in range(EXPERTS_PER_GROUP)]

    sub = lax.broadcasted_iota(jnp.int32, (8, tm), 0)
    cw8 = jnp.zeros((8, tm), _F32)
    for k in range(EXPERTS_PER_GROUP):
        cw8 = jnp.where(sub == k, cw[k], cw8)
    meta_t = jnp.concatenate([cw8, jnp.zeros((META_W - 8, tm), _F32)], axis=0)
    row_ref[:, D_MODEL:ROW_W] = meta_t.T

    onehot = jnp.zeros((8, tm), _F32)
    for g in range(N_GROUPS):
        onehot = jnp.where(jnp.logical_and(sub == g, sel[g]), 1.0, onehot)
    before = _dot(onehot.astype(_BF16), tri_ref[...])
    cnt = cnt_scr[...]
    rank = jnp.sum(onehot * (before + cnt[:, 0:1]), axis=0, keepdims=True)
    gidx = jnp.sum(onehot * sub.astype(_F32), axis=0, keepdims=True)
    pos_ref[0] = (gidx * float(n_tokens) + rank).astype(jnp.int32)
    cnt = cnt + jnp.sum(onehot, axis=1, keepdims=True)
    cnt_scr[...] = cnt
    cnt_ref[...] = cnt.astype(jnp.int32)


def _route(x1, mod, rwt, rb, tri, tokens_per_batch):
    t, d = x1.shape
    tm = ROUTE_TM
    per_b = tokens_per_batch // tm
    return pl.pallas_call(
        functools.partial(_route_kernel, n_tokens=t),
        grid=(t // tm,),
        in_specs=[pl.BlockSpec((tm, d), lambda j: (j, 0)),
                  pl.BlockSpec((1, 6, d), lambda j: (j // per_b, 0, 0)),
                  pl.BlockSpec((N_EXPERTS, d), lambda j: (0, 0)),
                  pl.BlockSpec((N_EXPERTS, 1), lambda j: (0, 0)),
                  pl.BlockSpec((tm, tm), lambda j: (0, 0))],
        out_specs=[pl.BlockSpec((tm, ROW_W), lambda j: (j, 0)),
                   pl.BlockSpec((1, 1, tm), lambda j: (j, 0, 0)),
                   pl.BlockSpec((8, 128), lambda j: (0, 0))],
        out_shape=[jax.ShapeDtypeStruct((t, ROW_W), _F32),
                   jax.ShapeDtypeStruct((t // tm, 1, tm), jnp.int32),
                   jax.ShapeDtypeStruct((8, 128), jnp.int32)],
        scratch_shapes=[pltpu.VMEM((8, 128), _F32)],
        name="route",
        compiler_params=pltpu.CompilerParams(
            dimension_semantics=("arbitrary",), vmem_limit_bytes=VMEM_LIMIT),
    )(x1, mod, rwt, rb, tri)


def _dispatch_kernel(pos_ref, cnt_ref, rows_ref, xs_ref, zero_scr, sem, *, n_tokens):
    td = rows_ref.shape[0]
    j = pl.program_id(0)
    base = j * td

    def row_copy(r, slot):
        return pltpu.make_async_copy(rows_ref.at[pl.ds(r, 1), :], xs_ref.at[pl.ds(slot, 1), :], sem)

    def start(r, carry):
        row_copy(r, pos_ref[base + r]).start()
        return carry

    def wait(r, carry):
        row_copy(0, 0).wait()
        return carry

    lax.fori_loop(0, td, start, 0, unroll=8)
    lax.fori_loop(0, td, wait, 0, unroll=8)

    @pl.when(j == pl.num_programs(0) - 1)
    def _():
        zero_scr[...] = jnp.zeros_like(zero_scr)

        def pad_copy(slot):
            return pltpu.make_async_copy(zero_scr.at[pl.ds(0, 1), :], xs_ref.at[pl.ds(slot, 1), :], sem)

        for g in range(N_GROUPS):
            c = cnt_ref[g]
            n_pad = (EXPERT_TM - c % EXPERT_TM) % EXPERT_TM

            def pstart(k, carry, c=c, g=g):
                pad_copy(g * n_tokens + c + k).start()
                return carry

            def pwait(k, carry):
                pad_copy(0).wait()
                return carry

            lax.fori_loop(0, n_pad, pstart, 0)
            lax.fori_loop(0, n_pad, pwait, 0)


def _dispatch(pos, cnt, rows):
    t = rows.shape[0]
    td = DISPATCH_TD
    grid_spec = pltpu.PrefetchScalarGridSpec(
        num_scalar_prefetch=2,
        grid=(t // td,),
        in_specs=[pl.BlockSpec((td, ROW_W), lambda j, p, c: (j, 0))],
        out_specs=pl.BlockSpec(memory_space=pl.ANY),
        scratch_shapes=[pltpu.VMEM((8, ROW_W), _F32), pltpu.SemaphoreType.DMA(())],
    )
    return pl.pallas_call(
        functools.partial(_dispatch_kernel, n_tokens=t),
        grid_spec=grid_spec,
        out_shape=jax.ShapeDtypeStruct((N_GROUPS * t, ROW_W), _F32),
        name="dispatch",
        compiler_params=pltpu.CompilerParams(
            dimension_semantics=("arbitrary",), vmem_limit_bytes=VMEM_LIMIT),
    )(pos, cnt, rows)


def _expert_kernel(blk_ref, grp_ref, live_ref, xs_ref, wg_ref, wu_ref, wd_ref, ys_ref):
    j = pl.program_id(0)

    @pl.when(live_ref[j] != 0)
    def _():
        x = xs_ref[:, 0:D_MODEL].astype(_BF16)
        acc = None
        for e in range(EXPERTS_PER_GROUP):
            gate = _dot(x, wg_ref[e])
            up = _dot(x, wu_ref[e])
            h = gate * jax.nn.sigmoid(gate) * up * xs_ref[:, D_MODEL + e:D_MODEL + e + 1]
            part = _dot(h.astype(_BF16), wd_ref[e])
            acc = part if acc is None else acc + part
        ys_ref[...] = acc


def _experts(blk, grp, live, xs, wg, wu, wd):
    tm = EXPERT_TM
    n_tiles = blk.shape[0]
    grid_spec = pltpu.PrefetchScalarGridSpec(
        num_scalar_prefetch=3,
        grid=(n_tiles,),
        in_specs=[
            pl.BlockSpec((tm, ROW_W), lambda j, b, g, v: (b[j], 0)),
            pl.BlockSpec((EXPERTS_PER_GROUP, D_MODEL, EXPERT_FF), lambda j, b, g, v: (g[j], 0, 0)),
            pl.BlockSpec((EXPERTS_PER_GROUP, D_MODEL, EXPERT_FF), lambda j, b, g, v: (g[j], 0, 0)),
            pl.BlockSpec((EXPERTS_PER_GROUP, EXPERT_FF, D_MODEL), lambda j, b, g, v: (g[j], 0, 0)),
        ],
        out_specs=pl.BlockSpec((tm, D_MODEL), lambda j, b, g, v: (b[j], 0)),
    )
    return pl.pallas_call(
        _expert_kernel,
        grid_spec=grid_spec,
        out_shape=jax.ShapeDtypeStruct((xs.shape[0], D_MODEL), _F32),
        name="experts",
        compiler_params=pltpu.CompilerParams(
            dimension_semantics=("arbitrary",), vmem_limit_bytes=VMEM_LIMIT),
    )(blk, grp, live, xs, wg, wu, wd)


def _tile_plan(cnt, n_tokens):
    tm = EXPERT_TM
    n_tiles = n_tokens // tm + N_GROUPS
    per_group = (cnt + tm - 1) // tm
    ends = jnp.cumsum(per_group)
    starts = ends - per_group
    j = jnp.arange(n_tiles, dtype=jnp.int32)
    total = ends[-1]
    jc = jnp.minimum(j, total - 1)
    grp = jnp.sum((jc[:, None] >= ends[None, :]).astype(jnp.int32), axis=1)
    blk = grp * (n_tokens // tm) + (jc - starts[grp])
    live = (j < total).astype(jnp.int32)
    return blk.astype(jnp.int32), grp.astype(jnp.int32), live


def _combine_kernel(pos_ref, x_ref, mod_ref, lng_ref, lnb_ref, ys_ref, o_ref, f_scr, sem):
    tf = x_ref.shape[0]
    j = pl.program_id(0)
    base = j * tf

    def row_copy(r, slot):
        return pltpu.make_async_copy(ys_ref.at[pl.ds(slot, 1), :], f_scr.at[pl.ds(r, 1), :], sem)

    def start(r, carry):
        row_copy(r, pos_ref[base + r]).start()
        return carry

    def wait(r, carry):
        row_copy(0, 0).wait()
        return carry

    lax.fori_loop(0, tf, start, 0, unroll=8)
    lax.fori_loop(0, tf, wait, 0, unroll=8)
    gate2 = mod_ref[0, 5:6, :]
    z = ALPHA * x_ref[...] + gate2 * f_scr[...]
    o_ref[...] = _layer_norm(z, lng_ref[...], lnb_ref[...])


def _combine(pos, x1, mod, lng, lnb, ys, tokens_per_batch):
    t, d = x1.shape
    tf = COMBINE_TF
    per_b = tokens_per_batch // tf
    grid_spec = pltpu.PrefetchScalarGridSpec(
        num_scalar_prefetch=1,
        grid=(t // tf,),
        in_specs=[pl.BlockSpec((tf, d), lambda j, p: (j, 0)),
                  pl.BlockSpec((1, 6, d), lambda j, p: (j // per_b, 0, 0)),
                  pl.BlockSpec((1, d), lambda j, p: (0, 0)),
                  pl.BlockSpec((1, d), lambda j, p: (0, 0)),
                  pl.BlockSpec(memory_space=pl.ANY)],
        out_specs=pl.BlockSpec((tf, d), lambda j, p: (j, 0)),
        scratch_shapes=[pltpu.VMEM((tf, d), _F32), pltpu.SemaphoreType.DMA(())],
    )
    return pl.pallas_call(
        _combine_kernel,
        grid_spec=grid_spec,
        out_shape=jax.ShapeDtypeStruct((t, d), _F32),
        name="combine",
        compiler_params=pltpu.CompilerParams(
            dimension_semantics=("arbitrary",), vmem_limit_bytes=VMEM_LIMIT),
    )(pos, x1, mod, lng, lnb, ys)


def _t5_causal_bucket(dist):
    max_exact = N_BUCKETS // 2
    d = np.maximum(dist, 0)
    df = np.maximum(d, 1).astype(np.float32)
    large = max_exact + (np.log(df / max_exact) / math.log(MAX_DISTANCE / max_exact)
                         * (N_BUCKETS - max_exact)).astype(np.int32)
    large = np.minimum(large, N_BUCKETS - 1)
    return np.where(d < max_exact, d, large).astype(np.int32)


def _head_bias(rel_bias):
    qi = np.arange(BLOCK)[:, None]
    sj = np.arange(2 * BLOCK)[None, :]
    dist = qi + BLOCK - sj
    in_window = (dist >= 0) & (dist < WINDOW)
    per_dist = rel_bias.astype(_F32)[_t5_causal_bucket(np.arange(WINDOW))]
    onehot = (np.clip(dist, 0, WINDOW - 1).reshape(-1)[:, None] == np.arange(WINDOW)[None, :])
    bias = jnp.dot(jnp.asarray(onehot, _F32), per_dist, precision=lax.Precision.HIGHEST)
    bias = jnp.where(in_window.reshape(-1)[:, None], bias, _NEG_INF)
    return jnp.transpose(bias).reshape(N_Q_HEADS, BLOCK, 2 * BLOCK)


def kernel(x, c, ada_w, ada_b, w_in, attn_sink, rel_bias, conv_w, conv_b, lru_wa, lru_ba, lru_wx,
           lru_bx, lru_lambda, w_out, ln1_g, ln1_b, router_w, router_b, moe_w_gate, moe_w_up,
           moe_w_down, ln2_g, ln2_b):
    b, s, d = x.shape
    t = b * s
    depth = w_in.shape[0]

    c_pad = jnp.pad(c, ((0, 8 - b), (0, 0)))
    mod_all = _modulation(c_pad, ada_w, ada_b)
    mod_all = mod_all[:, :b, :].reshape(depth, b, 6, d)

    biasp = _head_bias(rel_bias)
    rwt = router_w.T
    rb = router_b.reshape(N_EXPERTS, 1)
    tri = jnp.asarray(np.triu(np.ones((ROUTE_TM, ROUTE_TM), np.float32), 1), _BF16)

    for l in range(depth):
        mod = mod_all[l]
        wax = jnp.concatenate([lru_wa[l], lru_wx[l]], axis=-1).astype(_BF16)
        bax = jnp.stack([lru_ba[l], lru_bx[l]], axis=0)
        x1 = _mix(x, mod, attn_sink[l], w_in[l].astype(_BF16), biasp, conv_w[l],
                  conv_b[l].reshape(1, d), wax, bax, lru_lambda[l].reshape(1, d),
                  w_out[l].astype(_BF16), ln1_g[l].reshape(1, d), ln1_b[l].reshape(1, d))
        x1 = x1.reshape(t, d)
        rows, pos, cnt = _route(x1, mod, rwt, rb, tri, s)
        pos = pos.reshape(t)
        cnt = cnt[:N_GROUPS, 0]
        xs = _dispatch(pos, cnt, rows)
        blk, grp, live = _tile_plan(cnt, t)
        ys = _experts(blk, grp, live, xs, moe_w_gate[l].astype(_BF16), moe_w_up[l].astype(_BF16),
                      moe_w_down[l].astype(_BF16))
        x = _combine(pos, x1, mod, ln2_g[l].reshape(1, d), ln2_b[l].reshape(1, d), ys, s)
        x = x.reshape(b, s, d)
    return x
```

```python
import functools
import math

import jax
import jax.numpy as jnp
import numpy as np
from jax import lax
from jax.experimental import pallas as pl
from jax.experimental.pallas import tpu as pltpu

D_MODEL = 1024
DEPTH = 4
N_Q_HEADS = 16
N_KV_HEADS = 2
HEAD_DIM = 64
WINDOW = 128
BLOCK = 128
ATTN_W = N_Q_HEADS * HEAD_DIM
KV_W = N_KV_HEADS * HEAD_DIM
N_BUCKETS = 32
MAX_DISTANCE = 128
LRU_W = D_MODEL
LRU_BLOCKS = 8
LRU_BW = LRU_W // LRU_BLOCKS
CONV_W = 4
LRU_C = 8.0
N_EXPERTS = 16
N_GROUPS = 4
EXPERTS_PER_GROUP = N_EXPERTS // N_GROUPS
EXPERT_FF = 512
ALPHA = (2 * DEPTH) ** 0.25
LN_EPS = 1e-5

_Q0 = 0
_K0 = _Q0 + ATTN_W
_V0 = _K0 + KV_W
_LX0 = _V0 + KV_W
_LG0 = _LX0 + LRU_W
_GA0 = _LG0 + LRU_W
_GB0 = _GA0 + D_MODEL
IN_W = _GB0 + D_MODEL

N_PAIRS = N_Q_HEADS // 2
PAIRS_PER_KV = N_PAIRS // N_KV_HEADS
META_W = 128
ROW_W = D_MODEL + META_W

MIX_TS = 512
ROUTE_TM = 512
DISPATCH_TD = 512
EXPERT_TM = 256
COMBINE_TF = 512
VMEM_LIMIT = 56 * 1024 * 1024

_BF16 = jnp.bfloat16
_F32 = jnp.float32
_NEG_INF = float("-inf")


def _dot(a, b):
    return jnp.dot(a, b, preferred_element_type=_F32)


def _dot_nt(a, b, precision=None):
    return lax.dot_general(a, b, (((1,), (1,)), ((), ())), precision=precision,
                           preferred_element_type=_F32)


def _layer_norm(z, g, b):
    mu = jnp.mean(z, axis=-1, keepdims=True)
    zc = z - mu
    var = jnp.mean(zc * zc, axis=-1, keepdims=True)
    return zc * lax.rsqrt(var + LN_EPS) * g + b


def _mod_kernel(c_ref, w_ref, b_ref, o_ref):
    c = c_ref[...]
    c_act = c * jax.nn.sigmoid(c)
    o_ref[0] = _dot(c_act.astype(_BF16), w_ref[0].astype(_BF16)) + b_ref[0]


def _modulation(c_pad, ada_w, ada_b):
    depth, d, n = ada_w.shape
    tn = 1536
    rows = c_pad.shape[0]
    return pl.pallas_call(
        _mod_kernel,
        grid=(depth, n // tn),
        in_specs=[pl.BlockSpec((rows, d), lambda l, j: (0, 0)),
                  pl.BlockSpec((1, d, tn), lambda l, j: (l, 0, j)),
                  pl.BlockSpec((1, 1, tn), lambda l, j: (l, 0, j))],
        out_specs=pl.BlockSpec((1, rows, tn), lambda l, j: (l, 0, j)),
        out_shape=jax.ShapeDtypeStruct((depth, rows, n), _F32),
        name="modulation",
        compiler_params=pltpu.CompilerParams(vmem_limit_bytes=VMEM_LIMIT),
    )(c_pad, ada_w, ada_b.reshape(depth, 1, n))


def _mix_kernel(sink_ref, x_ref, mod_ref, win_ref, bias_ref, convw_ref, convb_ref, wax_ref,
                bax_ref, lam_ref, wout_ref, lng_ref, lnb_ref, o_ref,
                q_scr, k_scr, v_scr, ya_scr, lxprev_scr, h_scr, *, layer):
    ts = x_ref.shape[1]
    nblk = ts // BLOCK
    i = pl.program_id(1)

    @pl.when(i == 0)
    def _():
        k_scr[:, 0:BLOCK, :] = jnp.zeros((4, BLOCK, KV_W), _BF16)
        v_scr[:, 0:BLOCK, :] = jnp.zeros((4, BLOCK, 2 * KV_W), _BF16)
        lxprev_scr[...] = jnp.zeros_like(lxprev_scr)
        h_scr[...] = jnp.zeros_like(h_scr)

    x = x_ref[0]
    shift1 = mod_ref[0, 0:1, :]
    scale1 = mod_ref[0, 1:2, :]
    gate1 = mod_ref[0, 2:3, :]
    u = (x * (1.0 + scale1) + shift1).astype(_BF16)

    qkv = _dot(u, win_ref[:, _Q0:_LX0])
    q_scr[...] = (qkv[:, 0:ATTN_W] * (HEAD_DIM ** -0.5)).astype(_BF16)
    low = lax.broadcasted_iota(jnp.int32, (ts, KV_W), 1) < HEAD_DIM
    k_new = qkv[:, _K0:_K0 + KV_W]
    k_rolled = pltpu.roll(k_new, HEAD_DIM, 1)
    v_new = qkv[:, _V0:_V0 + KV_W]
    v_rolled = pltpu.roll(v_new, HEAD_DIM, 1)
    k_scr[0, BLOCK:BLOCK + ts, :] = jnp.where(low, k_new, 0.0).astype(_BF16)
    k_scr[1, BLOCK:BLOCK + ts, :] = jnp.where(low, 0.0, k_rolled).astype(_BF16)
    k_scr[2, BLOCK:BLOCK + ts, :] = jnp.where(low, k_rolled, 0.0).astype(_BF16)
    k_scr[3, BLOCK:BLOCK + ts, :] = jnp.where(low, 0.0, k_new).astype(_BF16)
    ones_lo = jnp.where(low, 1.0, 0.0).astype(_BF16)
    ones_hi = jnp.where(low, 0.0, 1.0).astype(_BF16)
    v_scr[0, BLOCK:BLOCK + ts, 0:KV_W] = jnp.where(low, v_new, 0.0).astype(_BF16)
    v_scr[1, BLOCK:BLOCK + ts, 0:KV_W] = jnp.where(low, 0.0, v_rolled).astype(_BF16)
    v_scr[2, BLOCK:BLOCK + ts, 0:KV_W] = jnp.where(low, v_rolled, 0.0).astype(_BF16)
    v_scr[3, BLOCK:BLOCK + ts, 0:KV_W] = jnp.where(low, 0.0, v_new).astype(_BF16)
    for slot in range(4):
        v_scr[slot, BLOCK:BLOCK + ts, KV_W:2 * KV_W] = ones_lo if slot % 2 == 0 else ones_hi

    prev_cols = lax.broadcasted_iota(jnp.int32, (BLOCK, 2 * BLOCK), 1) < BLOCK
    low_o = lax.broadcasted_iota(jnp.int32, (BLOCK, 2 * HEAD_DIM), 1) < HEAD_DIM

    def attn_block(jb, carry):
        r0 = pl.multiple_of(jb * BLOCK, BLOCK)
        first = (i * nblk + jb) == 0
        neg = jnp.where(jnp.logical_and(prev_cols, first), _NEG_INF, 0.0)
        for p in range(N_PAIRS):
            g = p // PAIRS_PER_KV
            qp = q_scr[pl.ds(r0, BLOCK), p * 128:(p + 1) * 128]
            acc = None
            sink_terms = []
            for hh in range(2):
                s = (_dot_nt(qp, k_scr[2 * g + hh, pl.ds(r0, 2 * BLOCK), :])
                     + bias_ref[2 * p + hh] + neg)
                sink = sink_ref[layer, 2 * p + hh]
                m = jnp.maximum(jnp.max(s, axis=-1, keepdims=True), sink)
                e = jnp.exp(s - m).astype(_BF16)
                sink_terms.append(jnp.exp(sink - m))
                part = _dot(e, v_scr[2 * g + hh, pl.ds(r0, 2 * BLOCK), :])
                acc = part if acc is None else acc + part
            den = acc[:, 2 * HEAD_DIM:] + jnp.where(low_o, sink_terms[0], sink_terms[1])
            ya_scr[pl.ds(r0, BLOCK), p * 128:(p + 1) * 128] = acc[:, 0:2 * HEAD_DIM] / den
        return carry

    lax.fori_loop(0, nblk, attn_block, 0)

    k_scr[:, 0:BLOCK, :] = k_scr[:, ts:ts + BLOCK, :]
    v_scr[:, 0:BLOCK, :] = v_scr[:, ts:ts + BLOCK, :]

    lx = _dot(u, win_ref[:, _LX0:_LG0])
    ext = jnp.concatenate([lxprev_scr[...], lx], axis=0)
    lxprev_scr[...] = lx[ts - 8:ts, :]
    xc = convb_ref[...] + convw_ref[CONV_W - 1:CONV_W, :] * lx
    for j in range(1, CONV_W):
        shifted = pltpu.roll(ext, j, 0)[8:8 + ts, :]
        xc = xc + convw_ref[CONV_W - 1 - j:CONV_W - j, :] * shifted
    xcb = xc.astype(_BF16)
    r_parts = []
    i_parts = []
    for hb in range(LRU_BLOCKS):
        ri = _dot(xcb[:, hb * LRU_BW:(hb + 1) * LRU_BW], wax_ref[hb])
        r_parts.append(ri[:, 0:LRU_BW])
        i_parts.append(ri[:, LRU_BW:2 * LRU_BW])
    r = jax.nn.sigmoid(jnp.concatenate(r_parts, axis=1) + bax_ref[0:1, :])
    ig = jax.nn.sigmoid(jnp.concatenate(i_parts, axis=1) + bax_ref[1:2, :])
    nlam = -lam_ref[...]
    softplus = jnp.maximum(nlam, 0.0) + jnp.log(1.0 + jnp.exp(-jnp.abs(nlam)))
    a = jnp.exp((-LRU_C) * r * softplus)
    om = 1.0 - a * a
    root = jnp.where(om > 0.0, om * lax.rsqrt(om), 0.0)
    bv = root * (ig * xc)

    row8 = lax.broadcasted_iota(jnp.int32, (ts, LRU_W), 0) % 8
    for d in (1, 2, 4):
        a_s = pltpu.roll(a, d, 0)
        b_s = pltpu.roll(bv, d, 0)
        inside = row8 >= d
        bv = jnp.where(inside, a * b_s + bv, bv)
        a = jnp.where(inside, a * a_s, a)
    h = h_scr[...]
    h_parts = []
    for g8 in range(ts // 8):
        hg = bv[8 * g8:8 * g8 + 8, :] + a[8 * g8:8 * g8 + 8, :] * h
        h_parts.append(hg)
        h = hg[7:8, :]
    h_scr[...] = h
    hseq = jnp.concatenate(h_parts, axis=0)

    yb = hseq * jax.nn.gelu(_dot(u, win_ref[:, _LG0:_GA0]))

    ga = jax.nn.sigmoid(_dot(u, win_ref[:, _GA0:_GB0]))
    gb = jax.nn.sigmoid(_dot(u, win_ref[:, _GB0:IN_W]))
    y = (ga * ya_scr[...] + gb * yb).astype(_BF16)
    z = ALPHA * x + gate1 * _dot(y, wout_ref[...])
    o_ref[0] = _layer_norm(z, lng_ref[...], lnb_ref[...])


def _mix(layer, x, mod_all, sink, win, bias, convw, convb, wax, bax, lam, wout, lng, lnb):
    b, s, d = x.shape
    ts = MIX_TS
    lay2 = lambda bi, i, sk: (layer, 0, 0)
    lay3 = lambda bi, i, sk: (layer, 0, 0, 0)
    once = pl.Buffered(1)
    grid_spec = pltpu.PrefetchScalarGridSpec(
        num_scalar_prefetch=1,
        grid=(b, s // ts),
        in_specs=[
            pl.BlockSpec((1, ts, d), lambda bi, i, sk: (bi, i, 0)),
            pl.BlockSpec((None, 1, 6, d), lambda bi, i, sk: (layer, bi, 0, 0)),
            pl.BlockSpec((None, d, IN_W), lay2, pipeline_mode=once),
            pl.BlockSpec((N_Q_HEADS, BLOCK, 2 * BLOCK), lambda bi, i, sk: (0, 0, 0),
                         pipeline_mode=once),
            pl.BlockSpec((None, CONV_W, d), lay2),
            pl.BlockSpec((None, 1, d), lay2),
            pl.BlockSpec((None, LRU_BLOCKS, LRU_BW, 2 * LRU_BW), lay3),
            pl.BlockSpec((None, 2, d), lay2),
            pl.BlockSpec((None, 1, d), lay2),
            pl.BlockSpec((None, d, d), lay2, pipeline_mode=once),
            pl.BlockSpec((None, 1, d), lay2),
            pl.BlockSpec((None, 1, d), lay2),
        ],
        out_specs=pl.BlockSpec((1, ts, d), lambda bi, i, sk: (bi, i, 0)),
        scratch_shapes=[
            pltpu.VMEM((ts, ATTN_W), _BF16),
            pltpu.VMEM((4, ts + BLOCK, KV_W), _BF16),
            pltpu.VMEM((4, ts + BLOCK, 2 * KV_W), _BF16),
            pltpu.VMEM((ts, ATTN_W), _F32),
            pltpu.VMEM((8, LRU_W), _F32),
            pltpu.VMEM((1, LRU_W), _F32),
        ],
    )
    return pl.pallas_call(
        functools.partial(_mix_kernel, layer=layer),
        grid_spec=grid_spec,
        out_shape=jax.ShapeDtypeStruct((b, s, d), _F32),
        name="mix",
        compiler_params=pltpu.CompilerParams(
            dimension_semantics=("arbitrary", "arbitrary"), vmem_limit_bytes=VMEM_LIMIT),
    )(sink, x, mod_all, win, bias, convw, convb, wax, bax, lam, wout, lng, lnb)


def _second_largest(v0, v1, v2, v3):
    hi1, lo1 = jnp.maximum(v0, v1), jnp.minimum(v0, v1)
    hi2, lo2 = jnp.maximum(v2, v3), jnp.minimum(v2, v3)
    return jnp.maximum(hi1, hi2), jnp.maximum(jnp.minimum(hi1, hi2), jnp.maximum(lo1, lo2))


def _route_kernel(x_ref, mod_ref, rwt_ref, rb_ref, tri_ref, row_ref, pos_ref, cnt_ref, cnt_scr,
                  *, n_tokens):
    tm = x_ref.shape[0]
    j = pl.program_id(0)

    @pl.when(j == 0)
    def _():
        cnt_scr[...] = jnp.zeros_like(cnt_scr)

    shift2 = mod_ref[0, 3:4, :]
    scale2 = mod_ref[0, 4:5, :]
    u = x_ref[...] * (1.0 + scale2) + shift2
    row_ref[:, 0:D_MODEL] = u

    logits = _dot_nt(rwt_ref[...], u, precision=lax.Precision.HIGHEST) + rb_ref[...]
    mx = jnp.max(logits, axis=0, keepdims=True)
    ex = jnp.exp(logits - mx)
    scores = ex / jnp.sum(ex, axis=0, keepdims=True)
    sc = [scores[e:e + 1, :] for e in range(N_EXPERTS)]

    gscore = []
    for g in range(N_GROUPS):
        top1, top2 = _second_largest(*sc[4 * g:4 * g + 4])
        gscore.append(top1 + top2)
    sel = []
    taken = None
    for g in range(N_GROUPS):
        best = None
        for g2 in range(g + 1, N_GROUPS):
            c = gscore[g] >= gscore[g2]
            best = c if best is None else jnp.logical_and(best, c)
        if best is None:
            best = jnp.ones_like(gscore[g], dtype=jnp.bool_)
        if taken is not None:
            best = jnp.logical_and(best, jnp.logical_not(taken))
        sel.append(best)
        taken = best if taken is None else jnp.logical_or(taken, best)

    vals = []
    for k in range(EXPERTS_PER_GROUP):
        v = sc[12 + k]
        for g in (2, 1, 0):
            v = jnp.where(sel[g], sc[4 * g + k], v)
        vals.append(v)
    ranks = []
    for k in range(EXPERTS_PER_GROUP):
        rk = jnp.zeros_like(vals[k])
        for k2 in range(EXPERTS_PER_GROUP):
            if k2 == k:
                continue
            beats = (vals[k2] >= vals[k]) if k2 < k else (vals[k2] > vals[k])
            rk = rk + jnp.where(beats, 1.0, 0.0)
        ranks.append(rk)
    top_a = jnp.zeros_like(vals[0])
    top_b = jnp.zeros_like(vals[0])
    for k in range(EXPERTS_PER_GROUP):
        top_a = jnp.where(ranks[k] == 0.0, vals[k], top_a)
        top_b = jnp.where(ranks[k] == 1.0, vals[k], top_b)
    denom = top_a + top_b
    cw = [jnp.where(ranks[k] < 2.0, vals[k] / denom, 0.0) for k in range(EXPERTS_PER_GROUP)]

    sub = lax.broadcasted_iota(jnp.int32, (8, tm), 0)
    cw8 = jnp.zeros((8, tm), _F32)
    for k in range(EXPERTS_PER_GROUP):
        cw8 = jnp.where(sub == k, cw[k], cw8)
    meta_t = jnp.concatenate([cw8, jnp.zeros((META_W - 8, tm), _F32)], axis=0)
    row_ref[:, D_MODEL:ROW_W] = meta_t.T

    onehot = jnp.zeros((8, tm), _F32)
    for g in range(N_GROUPS):
        onehot = jnp.where(jnp.logical_and(sub == g, sel[g]), 1.0, onehot)
    before = _dot(onehot.astype(_BF16), tri_ref[...])
    cnt = cnt_scr[...]
    rank = jnp.sum(onehot * (before + cnt[:, 0:1]), axis=0, keepdims=True)
    gidx = jnp.sum(onehot * sub.astype(_F32), axis=0, keepdims=True)
    pos_ref[0] = (gidx * float(n_tokens) + rank).astype(jnp.int32)
    cnt = cnt + jnp.sum(onehot, axis=1, keepdims=True)
    cnt_scr[...] = cnt
    cnt_ref[...] = cnt.astype(jnp.int32)


def _route(x1, mod, rwt, rb, tri, tokens_per_batch):
    t, d = x1.shape
    tm = ROUTE_TM
    per_b = tokens_per_batch // tm
    return pl.pallas_call(
        functools.partial(_route_kernel, n_tokens=t),
        grid=(t // tm,),
        in_specs=[pl.BlockSpec((tm, d), lambda j: (j, 0)),
                  pl.BlockSpec((1, 6, d), lambda j: (j // per_b, 0, 0)),
                  pl.BlockSpec((N_EXPERTS, d), lambda j: (0, 0)),
                  pl.BlockSpec((N_EXPERTS, 1), lambda j: (0, 0)),
                  pl.BlockSpec((tm, tm), lambda j: (0, 0))],
        out_specs=[pl.BlockSpec((tm, ROW_W), lambda j: (j, 0)),
                   pl.BlockSpec((1, 1, tm), lambda j: (j, 0, 0)),
                   pl.BlockSpec((8, 128), lambda j: (0, 0))],
        out_shape=[jax.ShapeDtypeStruct((t, ROW_W), _F32),
                   jax.ShapeDtypeStruct((t // tm, 1, tm), jnp.int32),
                   jax.ShapeDtypeStruct((8, 128), jnp.int32)],
        scratch_shapes=[pltpu.VMEM((8, 128), _F32)],
        name="route",
        compiler_params=pltpu.CompilerParams(
            dimension_semantics=("arbitrary",), vmem_limit_bytes=VMEM_LIMIT),
    )(x1, mod, rwt, rb, tri)


def _dispatch_kernel(pos_ref, cnt_ref, rows_ref, xs_ref, zero_scr, sem, *, n_tokens):
    td = rows_ref.shape[0]
    j = pl.program_id(0)
    base = j * td

    def row_copy(r, slot):
        return pltpu.make_async_copy(rows_ref.at[pl.ds(r, 1), :], xs_ref.at[pl.ds(slot, 1), :], sem)

    def start(r, carry):
        row_copy(r, pos_ref[base + r]).start()
        return carry

    def wait(r, carry):
        row_copy(0, 0).wait()
        return carry

    lax.fori_loop(0, td, start, 0, unroll=8)
    lax.fori_loop(0, td, wait, 0, unroll=8)

    @pl.when(j == pl.num_programs(0) - 1)
    def _():
        zero_scr[...] = jnp.zeros_like(zero_scr)

        def pad_copy(slot):
            return pltpu.make_async_copy(zero_scr.at[pl.ds(0, 1), :], xs_ref.at[pl.ds(slot, 1), :], sem)

        for g in range(N_GROUPS):
            c = cnt_ref[g]
            n_pad = (EXPERT_TM - c % EXPERT_TM) % EXPERT_TM

            def pstart(k, carry, c=c, g=g):
                pad_copy(g * n_tokens + c + k).start()
                return carry

            def pwait(k, carry):
                pad_copy(0).wait()
                return carry

            lax.fori_loop(0, n_pad, pstart, 0)
            lax.fori_loop(0, n_pad, pwait, 0)


def _dispatch(pos, cnt, rows):
    t = rows.shape[0]
    td = DISPATCH_TD
    grid_spec = pltpu.PrefetchScalarGridSpec(
        num_scalar_prefetch=2,
        grid=(t // td,),
        in_specs=[pl.BlockSpec((td, ROW_W), lambda j, p, c: (j, 0))],
        out_specs=pl.BlockSpec(memory_space=pl.ANY),
        scratch_shapes=[pltpu.VMEM((8, ROW_W), _F32), pltpu.SemaphoreType.DMA(())],
    )
    return pl.pallas_call(
        functools.partial(_dispatch_kernel, n_tokens=t),
        grid_spec=grid_spec,
        out_shape=jax.ShapeDtypeStruct((N_GROUPS * t, ROW_W), _F32),
        name="dispatch",
        compiler_params=pltpu.CompilerParams(
            dimension_semantics=("arbitrary",), vmem_limit_bytes=VMEM_LIMIT),
    )(pos, cnt, rows)


def _expert_kernel(blk_ref, grp_ref, live_ref, xs_ref, wg_ref, wu_ref, wd_ref, ys_ref):
    j = pl.program_id(0)

    @pl.when(live_ref[j] != 0)
    def _():
        x = xs_ref[:, 0:D_MODEL].astype(_BF16)
        acc = None
        for e in range(EXPERTS_PER_GROUP):
            gate = _dot(x, wg_ref[e])
            up = _dot(x, wu_ref[e])
            h = gate * jax.nn.sigmoid(gate) * up * xs_ref[:, D_MODEL + e:D_MODEL + e + 1]
            part = _dot(h.astype(_BF16), wd_ref[e])
            acc = part if acc is None else acc + part
        ys_ref[...] = acc


def _experts(layer, blk, grp, live, xs, wg, wu, wd):
    tm = EXPERT_TM
    n_tiles = blk.shape[0]
    grid_spec = pltpu.PrefetchScalarGridSpec(
        num_scalar_prefetch=3,
        grid=(n_tiles,),
        in_specs=[
            pl.BlockSpec((tm, ROW_W), lambda j, b, g, v: (b[j], 0)),
            pl.BlockSpec((None, EXPERTS_PER_GROUP, D_MODEL, EXPERT_FF),
                         lambda j, b, g, v: (layer, g[j], 0, 0)),
            pl.BlockSpec((None, EXPERTS_PER_GROUP, D_MODEL, EXPERT_FF),
                         lambda j, b, g, v: (layer, g[j], 0, 0)),
            pl.BlockSpec((None, EXPERTS_PER_GROUP, EXPERT_FF, D_MODEL),
                         lambda j, b, g, v: (layer, g[j], 0, 0)),
        ],
        out_specs=pl.BlockSpec((tm, D_MODEL), lambda j, b, g, v: (b[j], 0)),
    )
    return pl.pallas_call(
        _expert_kernel,
        grid_spec=grid_spec,
        out_shape=jax.ShapeDtypeStruct((xs.shape[0], D_MODEL), _F32),
        name="experts",
        compiler_params=pltpu.CompilerParams(
            dimension_semantics=("arbitrary",), vmem_limit_bytes=VMEM_LIMIT),
    )(blk, grp, live, xs, wg, wu, wd)


def _tile_plan(cnt, n_tokens):
    tm = EXPERT_TM
    n_tiles = n_tokens // tm + N_GROUPS
    per_group = (cnt + tm - 1) // tm
    ends = jnp.cumsum(per_group)
    starts = ends - per_group
    j = jnp.arange(n_tiles, dtype=jnp.int32)
    total = ends[-1]
    jc = jnp.minimum(j, total - 1)
    grp = jnp.sum((jc[:, None] >= ends[None, :]).astype(jnp.int32), axis=1)
    blk = grp * (n_tokens // tm) + (jc - starts[grp])
    live = (j < total).astype(jnp.int32)
    return blk.astype(jnp.int32), grp.astype(jnp.int32), live


def _combine_kernel(pos_ref, x_ref, mod_ref, lng_ref, lnb_ref, ys_ref, o_ref, f_scr, sem):
    tf = x_ref.shape[0]
    j = pl.program_id(0)
    base = j * tf

    def row_copy(r, slot):
        return pltpu.make_async_copy(ys_ref.at[pl.ds(slot, 1), :], f_scr.at[pl.ds(r, 1), :], sem)

    def start(r, carry):
        row_copy(r, pos_ref[base + r]).start()
        return carry

    def wait(r, carry):
        row_copy(0, 0).wait()
        return carry

    lax.fori_loop(0, tf, start, 0, unroll=8)
    lax.fori_loop(0, tf, wait, 0, unroll=8)
    gate2 = mod_ref[0, 5:6, :]
    z = ALPHA * x_ref[...] + gate2 * f_scr[...]
    o_ref[...] = _layer_norm(z, lng_ref[...], lnb_ref[...])


def _combine(pos, x1, mod, lng, lnb, ys, tokens_per_batch):
    t, d = x1.shape
    tf = COMBINE_TF
    per_b = tokens_per_batch // tf
    grid_spec = pltpu.PrefetchScalarGridSpec(
        num_scalar_prefetch=1,
        grid=(t // tf,),
        in_specs=[pl.BlockSpec((tf, d), lambda j, p: (j, 0)),
                  pl.BlockSpec((1, 6, d), lambda j, p: (j // per_b, 0, 0)),
                  pl.BlockSpec((1, d), lambda j, p: (0, 0)),
                  pl.BlockSpec((1, d), lambda j, p: (0, 0)),
                  pl.BlockSpec(memory_space=pl.ANY)],
        out_specs=pl.BlockSpec((tf, d), lambda j, p: (j, 0)),
        scratch_shapes=[pltpu.VMEM((tf, d), _F32), pltpu.SemaphoreType.DMA(())],
    )
    return pl.pallas_call(
        _combine_kernel,
        grid_spec=grid_spec,
        out_shape=jax.ShapeDtypeStruct((t, d), _F32),
        name="combine",
        compiler_params=pltpu.CompilerParams(
            dimension_semantics=("arbitrary",), vmem_limit_bytes=VMEM_LIMIT),
    )(pos, x1, mod, lng, lnb, ys)


def _t5_causal_bucket(dist):
    max_exact = N_BUCKETS // 2
    d = np.maximum(dist, 0)
    df = np.maximum(d, 1).astype(np.float32)
    large = max_exact + (np.log(df / max_exact) / math.log(MAX_DISTANCE / max_exact)
                         * (N_BUCKETS - max_exact)).astype(np.int32)
    large = np.minimum(large, N_BUCKETS - 1)
    return np.where(d < max_exact, d, large).astype(np.int32)


def _head_bias(rel_bias):
    qi = np.arange(BLOCK)[:, None]
    sj = np.arange(2 * BLOCK)[None, :]
    dist = qi + BLOCK - sj
    in_window = (dist >= 0) & (dist < WINDOW)
    per_dist = rel_bias.astype(_F32)[_t5_causal_bucket(np.arange(WINDOW))]
    onehot = (np.clip(dist, 0, WINDOW - 1).reshape(-1)[:, None] == np.arange(WINDOW)[None, :])
    bias = jnp.dot(jnp.asarray(onehot, _F32), per_dist, precision=lax.Precision.HIGHEST)
    bias = jnp.where(in_window.reshape(-1)[:, None], bias, _NEG_INF)
    return jnp.transpose(bias).reshape(N_Q_HEADS, BLOCK, 2 * BLOCK)


def kernel(x, c, ada_w, ada_b, w_in, attn_sink, rel_bias, conv_w, conv_b, lru_wa, lru_ba, lru_wx,
           lru_bx, lru_lambda, w_out, ln1_g, ln1_b, router_w, router_b, moe_w_gate, moe_w_up,
           moe_w_down, ln2_g, ln2_b):
    b, s, d = x.shape
    t = b * s
    depth = w_in.shape[0]

    c_pad = jnp.pad(c, ((0, 8 - b), (0, 0)))
    mod_all = _modulation(c_pad, ada_w, ada_b)
    mod_all = mod_all[:, :b, :].reshape(depth, b, 6, d)

    biasp = _head_bias(rel_bias)
    rwt = router_w.T
    rb = router_b.reshape(N_EXPERTS, 1)
    tri = jnp.asarray(np.triu(np.ones((ROUTE_TM, ROUTE_TM), np.float32), 1), _BF16)

    win_bf = w_in.astype(_BF16)
    wout_bf = w_out.astype(_BF16)
    wax = jnp.concatenate([lru_wa, lru_wx], axis=-1).astype(_BF16)
    bax = jnp.stack([lru_ba, lru_bx], axis=1)
    wg_bf = moe_w_gate.astype(_BF16)
    wu_bf = moe_w_up.astype(_BF16)
    wd_bf = moe_w_down.astype(_BF16)
    conv_b3 = conv_b.reshape(depth, 1, d)
    lam3 = lru_lambda.reshape(depth, 1, d)
    ln1_g3 = ln1_g.reshape(depth, 1, d)
    ln1_b3 = ln1_b.reshape(depth, 1, d)

    for l in range(depth):
        mod = mod_all[l]
        x1 = _mix(l, x, mod_all, attn_sink, win_bf, biasp, conv_w, conv_b3, wax, bax, lam3, wout_bf,
                  ln1_g3, ln1_b3)
        x1 = x1.reshape(t, d)
        rows, pos, cnt = _route(x1, mod, rwt, rb, tri, s)
        pos = pos.reshape(t)
        cnt = cnt[:N_GROUPS, 0]
        xs = _dispatch(pos, cnt, rows)
        blk, grp, live = _tile_plan(cnt, t)
        ys = _experts(l, blk, grp, live, xs, wg_bf, wu_bf, wd_bf)
        x = _combine(pos, x1, mod, ln2_g[l].reshape(1, d), ln2_b[l].reshape(1, d), ys, s)
        x = x.reshape(b, s, d)
    return x
```

```python
import functools
import math

import jax
import jax.numpy as jnp
import numpy as np
from jax import lax
from jax.experimental import pallas as pl
from jax.experimental.pallas import tpu as pltpu

D_MODEL = 1024
DEPTH = 4
N_Q_HEADS = 16
N_KV_HEADS = 2
HEAD_DIM = 64
WINDOW = 128
BLOCK = 128
ATTN_W = N_Q_HEADS * HEAD_DIM
KV_W = N_KV_HEADS * HEAD_DIM
N_BUCKETS = 32
MAX_DISTANCE = 128
LRU_W = D_MODEL
LRU_BLOCKS = 8
LRU_BW = LRU_W // LRU_BLOCKS
CONV_W = 4
LRU_C = 8.0
N_EXPERTS = 16
N_GROUPS = 4
EXPERTS_PER_GROUP = N_EXPERTS // N_GROUPS
EXPERT_FF = 512
ALPHA = (2 * DEPTH) ** 0.25
LN_EPS = 1e-5

_Q0 = 0
_K0 = _Q0 + ATTN_W
_V0 = _K0 + KV_W
_LX0 = _V0 + KV_W
_LG0 = _LX0 + LRU_W
_GA0 = _LG0 + LRU_W
_GB0 = _GA0 + D_MODEL
IN_W = _GB0 + D_MODEL

N_PAIRS = N_Q_HEADS // 2
PAIRS_PER_KV = N_PAIRS // N_KV_HEADS
META_W = 128
ROW_W = D_MODEL + META_W

MIX_TS = 512
ROUTE_TM = 512
EXPERT_TM = 256
TRASH_ROWS = N_GROUPS * EXPERT_TM
COMBINE_TF = 512
VMEM_LIMIT = 56 * 1024 * 1024

_BF16 = jnp.bfloat16
_F32 = jnp.float32
_NEG_INF = float("-inf")


def _dot(a, b):
    return jnp.dot(a, b, preferred_element_type=_F32)


def _dot_nt(a, b, precision=None):
    return lax.dot_general(a, b, (((1,), (1,)), ((), ())), precision=precision,
                           preferred_element_type=_F32)


def _layer_norm(z, g, b):
    mu = jnp.mean(z, axis=-1, keepdims=True)
    zc = z - mu
    var = jnp.mean(zc * zc, axis=-1, keepdims=True)
    return zc * lax.rsqrt(var + LN_EPS) * g + b


def _mod_kernel(c_ref, w_ref, b_ref, o_ref):
    c = c_ref[...]
    c_act = c * jax.nn.sigmoid(c)
    o_ref[0] = _dot(c_act.astype(_BF16), w_ref[0].astype(_BF16)) + b_ref[0]


def _modulation(c_pad, ada_w, ada_b):
    depth, d, n = ada_w.shape
    tn = 1536
    rows = c_pad.shape[0]
    return pl.pallas_call(
        _mod_kernel,
        grid=(depth, n // tn),
        in_specs=[pl.BlockSpec((rows, d), lambda l, j: (0, 0)),
                  pl.BlockSpec((1, d, tn), lambda l, j: (l, 0, j)),
                  pl.BlockSpec((1, 1, tn), lambda l, j: (l, 0, j))],
        out_specs=pl.BlockSpec((1, rows, tn), lambda l, j: (l, 0, j)),
        out_shape=jax.ShapeDtypeStruct((depth, rows, n), _F32),
        name="modulation",
        compiler_params=pltpu.CompilerParams(vmem_limit_bytes=VMEM_LIMIT),
    )(c_pad, ada_w, ada_b.reshape(depth, 1, n))


def _mix_kernel(sink_ref, x_ref, *refs, layer):
    if layer > 0:
        f_ref, modp_ref, ln2g_ref, ln2b_ref = refs[:4]
        refs = refs[4:]
    (mod_ref, win_ref, bias_ref, convw_ref, convb_ref, wax_ref, bax_ref, lam_ref, wout_ref, lng_ref,
     lnb_ref, o_ref, q_scr, k_scr, v_scr, ya_scr, lxprev_scr, h_scr) = refs
    ts = x_ref.shape[1]
    nblk = ts // BLOCK
    i = pl.program_id(1)

    @pl.when(i == 0)
    def _():
        k_scr[:, 0:BLOCK, :] = jnp.zeros((4, BLOCK, KV_W), _BF16)
        v_scr[:, 0:BLOCK, :] = jnp.zeros((4, BLOCK, 2 * KV_W), _BF16)
        lxprev_scr[...] = jnp.zeros_like(lxprev_scr)
        h_scr[...] = jnp.zeros_like(h_scr)

    x = x_ref[0]
    if layer > 0:
        x = _layer_norm(ALPHA * x + modp_ref[0, 5:6, :] * f_ref[...], ln2g_ref[...], ln2b_ref[...])
    shift1 = mod_ref[0, 0:1, :]
    scale1 = mod_ref[0, 1:2, :]
    gate1 = mod_ref[0, 2:3, :]
    u = (x * (1.0 + scale1) + shift1).astype(_BF16)

    qkv = _dot(u, win_ref[:, _Q0:_LX0])
    q_scr[...] = (qkv[:, 0:ATTN_W] * (HEAD_DIM ** -0.5)).astype(_BF16)
    low = lax.broadcasted_iota(jnp.int32, (ts, KV_W), 1) < HEAD_DIM
    k_new = qkv[:, _K0:_K0 + KV_W]
    k_rolled = pltpu.roll(k_new, HEAD_DIM, 1)
    v_new = qkv[:, _V0:_V0 + KV_W]
    v_rolled = pltpu.roll(v_new, HEAD_DIM, 1)
    k_scr[0, BLOCK:BLOCK + ts, :] = jnp.where(low, k_new, 0.0).astype(_BF16)
    k_scr[1, BLOCK:BLOCK + ts, :] = jnp.where(low, 0.0, k_rolled).astype(_BF16)
    k_scr[2, BLOCK:BLOCK + ts, :] = jnp.where(low, k_rolled, 0.0).astype(_BF16)
    k_scr[3, BLOCK:BLOCK + ts, :] = jnp.where(low, 0.0, k_new).astype(_BF16)
    ones_lo = jnp.where(low, 1.0, 0.0).astype(_BF16)
    ones_hi = jnp.where(low, 0.0, 1.0).astype(_BF16)
    v_scr[0, BLOCK:BLOCK + ts, 0:KV_W] = jnp.where(low, v_new, 0.0).astype(_BF16)
    v_scr[1, BLOCK:BLOCK + ts, 0:KV_W] = jnp.where(low, 0.0, v_rolled).astype(_BF16)
    v_scr[2, BLOCK:BLOCK + ts, 0:KV_W] = jnp.where(low, v_rolled, 0.0).astype(_BF16)
    v_scr[3, BLOCK:BLOCK + ts, 0:KV_W] = jnp.where(low, 0.0, v_new).astype(_BF16)
    for slot in range(4):
        v_scr[slot, BLOCK:BLOCK + ts, KV_W:2 * KV_W] = ones_lo if slot % 2 == 0 else ones_hi

    prev_cols = lax.broadcasted_iota(jnp.int32, (BLOCK, 2 * BLOCK), 1) < BLOCK
    low_o = lax.broadcasted_iota(jnp.int32, (BLOCK, 2 * HEAD_DIM), 1) < HEAD_DIM

    def attn_block(jb, carry):
        r0 = pl.multiple_of(jb * BLOCK, BLOCK)
        first = (i * nblk + jb) == 0
        neg = jnp.where(jnp.logical_and(prev_cols, first), _NEG_INF, 0.0)
        for p in range(N_PAIRS):
            g = p // PAIRS_PER_KV
            qp = q_scr[pl.ds(r0, BLOCK), p * 128:(p + 1) * 128]
            acc = None
            sink_terms = []
            for hh in range(2):
                s = (_dot_nt(qp, k_scr[2 * g + hh, pl.ds(r0, 2 * BLOCK), :])
                     + bias_ref[2 * p + hh] + neg)
                sink = sink_ref[layer, 2 * p + hh]
                m = jnp.maximum(jnp.max(s, axis=-1, keepdims=True), sink)
                e = jnp.exp(s - m).astype(_BF16)
                sink_terms.append(jnp.exp(sink - m))
                part = _dot(e, v_scr[2 * g + hh, pl.ds(r0, 2 * BLOCK), :])
                acc = part if acc is None else acc + part
            den = acc[:, 2 * HEAD_DIM:] + jnp.where(low_o, sink_terms[0], sink_terms[1])
            ya_scr[pl.ds(r0, BLOCK), p * 128:(p + 1) * 128] = acc[:, 0:2 * HEAD_DIM] / den
        return carry

    lax.fori_loop(0, nblk, attn_block, 0)

    k_scr[:, 0:BLOCK, :] = k_scr[:, ts:ts + BLOCK, :]
    v_scr[:, 0:BLOCK, :] = v_scr[:, ts:ts + BLOCK, :]

    lx = _dot(u, win_ref[:, _LX0:_LG0])
    ext = jnp.concatenate([lxprev_scr[...], lx], axis=0)
    lxprev_scr[...] = lx[ts - 8:ts, :]
    xc = convb_ref[...] + convw_ref[CONV_W - 1:CONV_W, :] * lx
    for j in range(1, CONV_W):
        shifted = pltpu.roll(ext, j, 0)[8:8 + ts, :]
        xc = xc + convw_ref[CONV_W - 1 - j:CONV_W - j, :] * shifted
    xcb = xc.astype(_BF16)
    r_parts = []
    i_parts = []
    for hb in range(LRU_BLOCKS):
        ri = _dot(xcb[:, hb * LRU_BW:(hb + 1) * LRU_BW], wax_ref[hb])
        r_parts.append(ri[:, 0:LRU_BW])
        i_parts.append(ri[:, LRU_BW:2 * LRU_BW])
    r = jax.nn.sigmoid(jnp.concatenate(r_parts, axis=1) + bax_ref[0:1, :])
    ig = jax.nn.sigmoid(jnp.concatenate(i_parts, axis=1) + bax_ref[1:2, :])
    nlam = -lam_ref[...]
    softplus = jnp.maximum(nlam, 0.0) + jnp.log(1.0 + jnp.exp(-jnp.abs(nlam)))
    a = jnp.exp((-LRU_C) * r * softplus)
    om = 1.0 - a * a
    root = jnp.where(om > 0.0, om * lax.rsqrt(om), 0.0)
    bv = root * (ig * xc)

    row8 = lax.broadcasted_iota(jnp.int32, (ts, LRU_W), 0) % 8
    for d in (1, 2, 4):
        a_s = pltpu.roll(a, d, 0)
        b_s = pltpu.roll(bv, d, 0)
        inside = row8 >= d
        bv = jnp.where(inside, a * b_s + bv, bv)
        a = jnp.where(inside, a * a_s, a)
    h = h_scr[...]
    h_parts = []
    for g8 in range(ts // 8):
        hg = bv[8 * g8:8 * g8 + 8, :] + a[8 * g8:8 * g8 + 8, :] * h
        h_parts.append(hg)
        h = hg[7:8, :]
    h_scr[...] = h
    hseq = jnp.concatenate(h_parts, axis=0)

    yb = hseq * jax.nn.gelu(_dot(u, win_ref[:, _LG0:_GA0]))

    ga = jax.nn.sigmoid(_dot(u, win_ref[:, _GA0:_GB0]))
    gb = jax.nn.sigmoid(_dot(u, win_ref[:, _GB0:IN_W]))
    y = (ga * ya_scr[...] + gb * yb).astype(_BF16)
    z = ALPHA * x + gate1 * _dot(y, wout_ref[...])
    o_ref[0] = _layer_norm(z, lng_ref[...], lnb_ref[...])


def _mix(layer, x, moe_out, mod_all, sink, win, bias, convw, convb, wax, bax, lam, wout, lng, lnb,
         ln2g, ln2b):
    b, s, d = x.shape
    ts = MIX_TS
    per_b = s // ts
    lay2 = lambda bi, i, sk: (layer, 0, 0)
    lay3 = lambda bi, i, sk: (layer, 0, 0, 0)
    prev2 = lambda bi, i, sk: (layer - 1, 0, 0)
    once = pl.Buffered(1)
    in_specs = [pl.BlockSpec((1, ts, d), lambda bi, i, sk: (bi, i, 0))]
    operands = [x]
    if layer > 0:
        in_specs += [
            pl.BlockSpec((ts, d), lambda bi, i, sk: (bi * per_b + i, 0)),
            pl.BlockSpec((None, 1, 6, d), lambda bi, i, sk: (layer - 1, bi, 0, 0)),
            pl.BlockSpec((None, 1, d), prev2),
            pl.BlockSpec((None, 1, d), prev2),
        ]
        operands += [moe_out, mod_all, ln2g, ln2b]
    in_specs += [
        pl.BlockSpec((None, 1, 6, d), lambda bi, i, sk: (layer, bi, 0, 0)),
        pl.BlockSpec((None, d, IN_W), lay2, pipeline_mode=once),
        pl.BlockSpec((N_Q_HEADS, BLOCK, 2 * BLOCK), lambda bi, i, sk: (0, 0, 0), pipeline_mode=once),
        pl.BlockSpec((None, CONV_W, d), lay2),
        pl.BlockSpec((None, 1, d), lay2),
        pl.BlockSpec((None, LRU_BLOCKS, LRU_BW, 2 * LRU_BW), lay3),
        pl.BlockSpec((None, 2, d), lay2),
        pl.BlockSpec((None, 1, d), lay2),
        pl.BlockSpec((None, d, d), lay2, pipeline_mode=once),
        pl.BlockSpec((None, 1, d), lay2),
        pl.BlockSpec((None, 1, d), lay2),
    ]
    operands += [mod_all, win, bias, convw, convb, wax, bax, lam, wout, lng, lnb]
    grid_spec = pltpu.PrefetchScalarGridSpec(
        num_scalar_prefetch=1,
        grid=(b, per_b),
        in_specs=in_specs,
        out_specs=pl.BlockSpec((1, ts, d), lambda bi, i, sk: (bi, i, 0)),
        scratch_shapes=[
            pltpu.VMEM((ts, ATTN_W), _BF16),
            pltpu.VMEM((4, ts + BLOCK, KV_W), _BF16),
            pltpu.VMEM((4, ts + BLOCK, 2 * KV_W), _BF16),
            pltpu.VMEM((ts, ATTN_W), _F32),
            pltpu.VMEM((8, LRU_W), _F32),
            pltpu.VMEM((1, LRU_W), _F32),
        ],
    )
    return pl.pallas_call(
        functools.partial(_mix_kernel, layer=layer),
        grid_spec=grid_spec,
        out_shape=jax.ShapeDtypeStruct((b, s, d), _F32),
        name="mix",
        compiler_params=pltpu.CompilerParams(
            dimension_semantics=("arbitrary", "arbitrary"), vmem_limit_bytes=VMEM_LIMIT),
    )(sink, *operands)


def _second_largest(v0, v1, v2, v3):
    hi1, lo1 = jnp.maximum(v0, v1), jnp.minimum(v0, v1)
    hi2, lo2 = jnp.maximum(v2, v3), jnp.minimum(v2, v3)
    return jnp.maximum(hi1, hi2), jnp.maximum(jnp.minimum(hi1, hi2), jnp.maximum(lo1, lo2))


def _route_kernel(x_ref, mod_ref, rwt_ref, rb_ref, tri_ref, row_ref, pos_ref, cnt_ref, cnt_scr,
                  *, n_tokens):
    tm = x_ref.shape[0]
    j = pl.program_id(0)

    @pl.when(j == 0)
    def _():
        cnt_scr[...] = jnp.zeros_like(cnt_scr)

    shift2 = mod_ref[0, 3:4, :]
    scale2 = mod_ref[0, 4:5, :]
    u = x_ref[...] * (1.0 + scale2) + shift2
    row_ref[:, 0:D_MODEL] = u

    logits = _dot_nt(rwt_ref[...], u, precision=lax.Precision.HIGHEST) + rb_ref[...]
    mx = jnp.max(logits, axis=0, keepdims=True)
    ex = jnp.exp(logits - mx)
    scores = ex / jnp.sum(ex, axis=0, keepdims=True)
    sc = [scores[e:e + 1, :] for e in range(N_EXPERTS)]

    gscore = []
    for g in range(N_GROUPS):
        top1, top2 = _second_largest(*sc[4 * g:4 * g + 4])
        gscore.append(top1 + top2)
    sel = []
    taken = None
    for g in range(N_GROUPS):
        best = None
        for g2 in range(g + 1, N_GROUPS):
            c = gscore[g] >= gscore[g2]
            best = c if best is None else jnp.logical_and(best, c)
        if best is None:
            best = jnp.ones_like(gscore[g], dtype=jnp.bool_)
        if taken is not None:
            best = jnp.logical_and(best, jnp.logical_not(taken))
        sel.append(best)
        taken = best if taken is None else jnp.logical_or(taken, best)

    vals = []
    for k in range(EXPERTS_PER_GROUP):
        v = sc[12 + k]
        for g in (2, 1, 0):
            v = jnp.where(sel[g], sc[4 * g + k], v)
        vals.append(v)
    ranks = []
    for k in range(EXPERTS_PER_GROUP):
        rk = jnp.zeros_like(vals[k])
        for k2 in range(EXPERTS_PER_GROUP):
            if k2 == k:
                continue
            beats = (vals[k2] >= vals[k]) if k2 < k else (vals[k2] > vals[k])
            rk = rk + jnp.where(beats, 1.0, 0.0)
        ranks.append(rk)
    top_a = jnp.zeros_like(vals[0])
    top_b = jnp.zeros_like(vals[0])
    for k in range(EXPERTS_PER_GROUP):
        top_a = jnp.where(ranks[k] == 0.0, vals[k], top_a)
        top_b = jnp.where(ranks[k] == 1.0, vals[k], top_b)
    denom = top_a + top_b
    cw = [jnp.where(ranks[k] < 2.0, vals[k] / denom, 0.0) for k in range(EXPERTS_PER_GROUP)]

    sub = lax.broadcasted_iota(jnp.int32, (8, tm), 0)
    cw8 = jnp.zeros((8, tm), _F32)
    for k in range(EXPERTS_PER_GROUP):
        cw8 = jnp.where(sub == k, cw[k], cw8)
    meta_t = jnp.concatenate([cw8, jnp.zeros((META_W - 8, tm), _F32)], axis=0)
    row_ref[:, D_MODEL:ROW_W] = meta_t.T

    onehot = jnp.zeros((8, tm), _F32)
    for g in range(N_GROUPS):
        onehot = jnp.where(jnp.logical_and(sub == g, sel[g]), 1.0, onehot)
    before = _dot(onehot.astype(_BF16), tri_ref[...])
    cnt = cnt_scr[...]
    rank = jnp.sum(onehot * (before + cnt[:, 0:1]), axis=0, keepdims=True)
    gidx = jnp.sum(onehot * sub.astype(_F32), axis=0, keepdims=True)
    pos_ref[0] = (gidx * float(n_tokens) + rank).astype(jnp.int32)
    cnt = cnt + jnp.sum(onehot, axis=1, keepdims=True)
    cnt_scr[...] = cnt
    cnt_ref[...] = cnt.astype(jnp.int32)


def _route(x1, mod, rwt, rb, tri, tokens_per_batch):
    t, d = x1.shape
    tm = ROUTE_TM
    per_b = tokens_per_batch // tm
    return pl.pallas_call(
        functools.partial(_route_kernel, n_tokens=t),
        grid=(t // tm,),
        in_specs=[pl.BlockSpec((tm, d), lambda j: (j, 0)),
                  pl.BlockSpec((1, 6, d), lambda j: (j // per_b, 0, 0)),
                  pl.BlockSpec((N_EXPERTS, d), lambda j: (0, 0)),
                  pl.BlockSpec((N_EXPERTS, 1), lambda j: (0, 0)),
                  pl.BlockSpec((tm, tm), lambda j: (0, 0))],
        out_specs=[pl.BlockSpec((tm, ROW_W), lambda j: (j, 0)),
                   pl.BlockSpec((1, 1, tm), lambda j: (j, 0, 0)),
                   pl.BlockSpec((8, 128), lambda j: (0, 0))],
        out_shape=[jax.ShapeDtypeStruct((t, ROW_W), _F32),
                   jax.ShapeDtypeStruct((t // tm, 1, tm), jnp.int32),
                   jax.ShapeDtypeStruct((8, 128), jnp.int32)],
        scratch_shapes=[pltpu.VMEM((8, 128), _F32)],
        name="route",
        compiler_params=pltpu.CompilerParams(
            dimension_semantics=("arbitrary",), vmem_limit_bytes=VMEM_LIMIT),
    )(x1, mod, rwt, rb, tri)


def _slots_kernel(pos_ref, cnt_ref, src_ref, *, n_tokens):
    tm = EXPERT_TM

    def fill(s, carry):
        src_ref[s] = n_tokens + lax.rem(s, TRASH_ROWS)
        return carry

    bases = []
    base = jnp.int32(0)
    for g in range(N_GROUPS):
        c = cnt_ref[g]
        cap = ((c + (tm - 1)) // tm) * tm
        bases.append(base)
        lax.fori_loop(base + c, base + cap, fill, 0)
        base = base + cap
    lax.fori_loop(base, base + tm, fill, 0)

    def place(t, carry):
        p = pos_ref[t]
        g = ((p >= n_tokens).astype(jnp.int32) + (p >= 2 * n_tokens).astype(jnp.int32)
             + (p >= 3 * n_tokens).astype(jnp.int32))
        b = jnp.where(g == 0, bases[0], jnp.where(g == 1, bases[1], jnp.where(g == 2, bases[2], bases[3])))
        src_ref[b + p - g * n_tokens] = t
        return carry

    lax.fori_loop(0, n_tokens, place, 0, unroll=8)


def _slots(pos, cnt):
    t = pos.shape[0]
    grid_spec = pltpu.PrefetchScalarGridSpec(
        num_scalar_prefetch=2,
        grid=(1,),
        in_specs=[],
        out_specs=pl.BlockSpec(memory_space=pltpu.SMEM),
    )
    return pl.pallas_call(
        functools.partial(_slots_kernel, n_tokens=t),
        grid_spec=grid_spec,
        out_shape=jax.ShapeDtypeStruct((t + TRASH_ROWS,), jnp.int32),
        name="slots",
        compiler_params=pltpu.CompilerParams(dimension_semantics=("arbitrary",)),
    )(pos, cnt)


def _expert_kernel(src_ref, grp_ref, live_ref, rows_ref, wg_ref, wu_ref, wd_ref, f_ref,
                   x_even, x_odd, y_even, y_odd, gsem, ssem, *, n_tokens):
    tm = x_even.shape[0]
    j = pl.program_id(0)
    x_bufs = (x_even, x_odd)
    y_bufs = (y_even, y_odd)

    def gather_row(tile, par, r):
        tok = jnp.minimum(src_ref[tile * tm + r], n_tokens - 1)
        return pltpu.make_async_copy(rows_ref.at[pl.ds(tok, 1), :], x_bufs[par].at[pl.ds(r, 1), :],
                                     gsem.at[par])

    def scatter_row(tile, par, r):
        dst = jnp.where(tile < 0, n_tokens + r, src_ref[jnp.maximum(tile, 0) * tm + r])
        return pltpu.make_async_copy(y_bufs[par].at[pl.ds(r, 1), :], f_ref.at[pl.ds(dst, 1), :],
                                     ssem.at[par])

    def start_gather(tile, par):
        for r in range(tm):
            gather_row(tile, par, r).start()

    def wait_gather(par):
        for r in range(tm):
            gather_row(0, par, r).wait()

    def start_scatter(tile, par):
        for r in range(tm):
            scatter_row(tile, par, r).start()

    def wait_scatter(par):
        for r in range(tm):
            scatter_row(0, par, r).wait()

    live = live_ref[j] != 0
    prev_live = jnp.logical_and(j >= 1, live_ref[jnp.maximum(j - 1, 0)] != 0)

    @pl.when(j == 0)
    def _():
        start_gather(0, 0)
        y_odd[...] = jnp.zeros_like(y_odd)

    for par in range(2):
        other = 1 - par
        mine = (j % 2) == par

        @pl.when(jnp.logical_and(live, mine))
        def _(par=par, other=other):
            wait_gather(par)

            @pl.when(j >= 1)
            def _():
                wait_scatter(par)

            n_share = 3 * EXPERTS_PER_GROUP
            share = -(-tm // n_share)
            issued = [0]

            def issue_share():
                lo = issued[0]
                hi = min(lo + share, tm)
                for r in range(lo, hi):
                    gather_row(j + 1, other, r).start()
                    scatter_row(j - 1, other, r).start()
                issued[0] = hi

            x_ref = x_bufs[par]
            x = x_ref[:, 0:D_MODEL].astype(_BF16)
            acc = None
            for e in range(EXPERTS_PER_GROUP):
                issue_share()
                gate = _dot(x, wg_ref[e])
                issue_share()
                up = _dot(x, wu_ref[e])
                h = gate * jax.nn.sigmoid(gate) * up * x_ref[:, D_MODEL + e:D_MODEL + e + 1]
                issue_share()
                part = _dot(h.astype(_BF16), wd_ref[e])
                acc = part if acc is None else acc + part
            assert issued[0] == tm
            y_bufs[par][...] = acc

        @pl.when(jnp.logical_and(jnp.logical_and(jnp.logical_not(live), prev_live), mine))
        def _(par=par, other=other):
            wait_gather(par)
            wait_scatter(par)
            start_scatter(j - 1, other)
            wait_scatter(other)


def _experts(layer, src, grp, live, rows, wg, wu, wd):
    tm = EXPERT_TM
    t = rows.shape[0]
    n_tiles = grp.shape[0]
    grid_spec = pltpu.PrefetchScalarGridSpec(
        num_scalar_prefetch=3,
        grid=(n_tiles,),
        in_specs=[
            pl.BlockSpec(memory_space=pl.ANY),
            pl.BlockSpec((None, EXPERTS_PER_GROUP, D_MODEL, EXPERT_FF),
                         lambda j, s, g, v: (layer, g[j], 0, 0)),
            pl.BlockSpec((None, EXPERTS_PER_GROUP, D_MODEL, EXPERT_FF),
                         lambda j, s, g, v: (layer, g[j], 0, 0)),
            pl.BlockSpec((None, EXPERTS_PER_GROUP, EXPERT_FF, D_MODEL),
                         lambda j, s, g, v: (layer, g[j], 0, 0)),
        ],
        out_specs=pl.BlockSpec(memory_space=pl.ANY),
        scratch_shapes=[
            pltpu.VMEM((tm, ROW_W), _F32),
            pltpu.VMEM((tm, ROW_W), _F32),
            pltpu.VMEM((tm, D_MODEL), _F32),
            pltpu.VMEM((tm, D_MODEL), _F32),
            pltpu.SemaphoreType.DMA((2,)),
            pltpu.SemaphoreType.DMA((2,)),
        ],
    )
    return pl.pallas_call(
        functools.partial(_expert_kernel, n_tokens=t),
        grid_spec=grid_spec,
        out_shape=jax.ShapeDtypeStruct((t + TRASH_ROWS, D_MODEL), _F32),
        name="experts",
        compiler_params=pltpu.CompilerParams(
            dimension_semantics=("arbitrary",), vmem_limit_bytes=VMEM_LIMIT),
    )(src, grp, live, rows, wg, wu, wd)


def _tile_plan(cnt, n_tokens):
    tm = EXPERT_TM
    n_tiles = n_tokens // tm + N_GROUPS
    ends = jnp.cumsum((cnt + tm - 1) // tm)
    j = jnp.arange(n_tiles, dtype=jnp.int32)
    grp = jnp.minimum(jnp.sum((j[:, None] >= ends[None, :]).astype(jnp.int32), axis=1), N_GROUPS - 1)
    live = (j < ends[-1]).astype(jnp.int32)
    return grp.astype(jnp.int32), live


def _final_norm_kernel(x_ref, f_ref, mod_ref, lng_ref, lnb_ref, o_ref):
    z = ALPHA * x_ref[...] + mod_ref[0, 5:6, :] * f_ref[...]
    o_ref[...] = _layer_norm(z, lng_ref[...], lnb_ref[...])


def _final_norm(x1, moe_out, mod, lng, lnb, tokens_per_batch):
    t, d = x1.shape
    tf = COMBINE_TF
    per_b = tokens_per_batch // tf
    return pl.pallas_call(
        _final_norm_kernel,
        grid=(t // tf,),
        in_specs=[pl.BlockSpec((tf, d), lambda j: (j, 0)),
                  pl.BlockSpec((tf, d), lambda j: (j, 0)),
                  pl.BlockSpec((1, 6, d), lambda j: (j // per_b, 0, 0)),
                  pl.BlockSpec((1, d), lambda j: (0, 0)),
                  pl.BlockSpec((1, d), lambda j: (0, 0))],
        out_specs=pl.BlockSpec((tf, d), lambda j: (j, 0)),
        out_shape=jax.ShapeDtypeStruct((t, d), _F32),
        name="final_norm",
        compiler_params=pltpu.CompilerParams(
            dimension_semantics=("arbitrary",), vmem_limit_bytes=VMEM_LIMIT),
    )(x1, moe_out, mod, lng, lnb)


def _t5_causal_bucket(dist):
    max_exact = N_BUCKETS // 2
    d = np.maximum(dist, 0)
    df = np.maximum(d, 1).astype(np.float32)
    large = max_exact + (np.log(df / max_exact) / math.log(MAX_DISTANCE / max_exact)
                         * (N_BUCKETS - max_exact)).astype(np.int32)
    large = np.minimum(large, N_BUCKETS - 1)
    return np.where(d < max_exact, d, large).astype(np.int32)


def _head_bias(rel_bias):
    qi = np.arange(BLOCK)[:, None]
    sj = np.arange(2 * BLOCK)[None, :]
    dist = qi + BLOCK - sj
    in_window = (dist >= 0) & (dist < WINDOW)
    per_dist = rel_bias.astype(_F32)[_t5_causal_bucket(np.arange(WINDOW))]
    onehot = (np.clip(dist, 0, WINDOW - 1).reshape(-1)[:, None] == np.arange(WINDOW)[None, :])
    bias = jnp.dot(jnp.asarray(onehot, _F32), per_dist, precision=lax.Precision.HIGHEST)
    bias = jnp.where(in_window.reshape(-1)[:, None], bias, _NEG_INF)
    return jnp.transpose(bias).reshape(N_Q_HEADS, BLOCK, 2 * BLOCK)


def kernel(x, c, ada_w, ada_b, w_in, attn_sink, rel_bias, conv_w, conv_b, lru_wa, lru_ba, lru_wx,
           lru_bx, lru_lambda, w_out, ln1_g, ln1_b, router_w, router_b, moe_w_gate, moe_w_up,
           moe_w_down, ln2_g, ln2_b):
    b, s, d = x.shape
    t = b * s
    depth = w_in.shape[0]

    c_pad = jnp.pad(c, ((0, 8 - b), (0, 0)))
    mod_all = _modulation(c_pad, ada_w, ada_b)
    mod_all = mod_all[:, :b, :].reshape(depth, b, 6, d)

    biasp = _head_bias(rel_bias)
    rwt = router_w.T
    rb = router_b.reshape(N_EXPERTS, 1)
    tri = jnp.asarray(np.triu(np.ones((ROUTE_TM, ROUTE_TM), np.float32), 1), _BF16)

    win_bf = w_in.astype(_BF16)
    wout_bf = w_out.astype(_BF16)
    wax = jnp.concatenate([lru_wa, lru_wx], axis=-1).astype(_BF16)
    bax = jnp.stack([lru_ba, lru_bx], axis=1)
    wg_bf = moe_w_gate.astype(_BF16)
    wu_bf = moe_w_up.astype(_BF16)
    wd_bf = moe_w_down.astype(_BF16)
    conv_b3 = conv_b.reshape(depth, 1, d)
    lam3 = lru_lambda.reshape(depth, 1, d)
    ln1_g3 = ln1_g.reshape(depth, 1, d)
    ln1_b3 = ln1_b.reshape(depth, 1, d)

    ln2_g3 = ln2_g.reshape(depth, 1, d)
    ln2_b3 = ln2_b.reshape(depth, 1, d)

    moe_out = None
    for l in range(depth):
        x = _mix(l, x, moe_out, mod_all, attn_sink, win_bf, biasp, conv_w, conv_b3, wax, bax, lam3,
                 wout_bf, ln1_g3, ln1_b3, ln2_g3, ln2_b3)
        rows, pos, cnt = _route(x.reshape(t, d), mod_all[l], rwt, rb, tri, s)
        cnt = cnt[:N_GROUPS, 0]
        src = _slots(pos.reshape(t), cnt)
        grp, live = _tile_plan(cnt, t)
        moe_out = _experts(l, src, grp, live, rows, wg_bf, wu_bf, wd_bf)
    out = _final_norm(x.reshape(t, d), moe_out, mod_all[depth - 1], ln2_g3[depth - 1], ln2_b3[depth - 1], s)
    return out.reshape(b, s, d)
```

```python
import functools
import math

import jax
import jax.numpy as jnp
import numpy as np
from jax import lax
from jax.experimental import pallas as pl
from jax.experimental.pallas import tpu as pltpu

D_MODEL = 1024
DEPTH = 4
N_Q_HEADS = 16
N_KV_HEADS = 2
HEAD_DIM = 64
WINDOW = 128
BLOCK = 128
ATTN_W = N_Q_HEADS * HEAD_DIM
KV_W = N_KV_HEADS * HEAD_DIM
N_BUCKETS = 32
MAX_DISTANCE = 128
LRU_W = D_MODEL
LRU_BLOCKS = 8
LRU_BW = LRU_W // LRU_BLOCKS
CONV_W = 4
LRU_C = 8.0
N_EXPERTS = 16
N_GROUPS = 4
EXPERTS_PER_GROUP = N_EXPERTS // N_GROUPS
EXPERT_FF = 512
ALPHA = (2 * DEPTH) ** 0.25
LN_EPS = 1e-5

_Q0 = 0
_K0 = _Q0 + ATTN_W
_V0 = _K0 + KV_W
_LX0 = _V0 + KV_W
_LG0 = _LX0 + LRU_W
_GA0 = _LG0 + LRU_W
_GB0 = _GA0 + D_MODEL
IN_W = _GB0 + D_MODEL

N_PAIRS = N_Q_HEADS // 2
PAIRS_PER_KV = N_PAIRS // N_KV_HEADS
META_W = 128
ROW_W = D_MODEL + META_W

MIX_TS = 512
ROUTE_TM = 512
EXPERT_TM = 256
TRASH_ROWS = N_GROUPS * EXPERT_TM
COMBINE_TF = 512
VMEM_LIMIT = 56 * 1024 * 1024

_BF16 = jnp.bfloat16
_F32 = jnp.float32
_NEG_INF = float("-inf")


def _dot(a, b):
    return jnp.dot(a, b, preferred_element_type=_F32)


def _dot_nt(a, b, precision=None):
    return lax.dot_general(a, b, (((1,), (1,)), ((), ())), precision=precision,
                           preferred_element_type=_F32)


def _layer_norm(z, g, b):
    mu = jnp.mean(z, axis=-1, keepdims=True)
    zc = z - mu
    var = jnp.mean(zc * zc, axis=-1, keepdims=True)
    return zc * lax.rsqrt(var + LN_EPS) * g + b


def _mod_kernel(c_ref, w_ref, b_ref, o_ref):
    c = c_ref[...]
    c_act = c * jax.nn.sigmoid(c)
    o_ref[0] = _dot(c_act.astype(_BF16), w_ref[0].astype(_BF16)) + b_ref[0]


def _modulation(c_pad, ada_w, ada_b):
    depth, d, n = ada_w.shape
    tn = 1536
    rows = c_pad.shape[0]
    return pl.pallas_call(
        _mod_kernel,
        grid=(depth, n // tn),
        in_specs=[pl.BlockSpec((rows, d), lambda l, j: (0, 0)),
                  pl.BlockSpec((1, d, tn), lambda l, j: (l, 0, j)),
                  pl.BlockSpec((1, 1, tn), lambda l, j: (l, 0, j))],
        out_specs=pl.BlockSpec((1, rows, tn), lambda l, j: (l, 0, j)),
        out_shape=jax.ShapeDtypeStruct((depth, rows, n), _F32),
        name="modulation",
        compiler_params=pltpu.CompilerParams(vmem_limit_bytes=VMEM_LIMIT),
    )(c_pad, ada_w, ada_b.reshape(depth, 1, n))


def _mix_kernel(sink_ref, x_ref, *refs, layer):
    if layer > 0:
        f_ref, modp_ref, ln2g_ref, ln2b_ref = refs[:4]
        refs = refs[4:]
    (mod_ref, win_ref, bias_ref, convw_ref, convb_ref, wax_ref, bax_ref, lam_ref, wout_ref, lng_ref,
     lnb_ref, o_ref, q_scr, k_scr, v_scr, ya_scr, lxprev_scr, h_scr) = refs
    ts = x_ref.shape[1]
    nblk = ts // BLOCK
    i = pl.program_id(1)

    @pl.when(i == 0)
    def _():
        k_scr[:, 0:BLOCK, :] = jnp.zeros((4, BLOCK, KV_W), _BF16)
        v_scr[:, 0:BLOCK, :] = jnp.zeros((4, BLOCK, 2 * KV_W), _BF16)
        lxprev_scr[...] = jnp.zeros_like(lxprev_scr)
        h_scr[...] = jnp.zeros_like(h_scr)

    x = x_ref[0]
    if layer > 0:
        x = _layer_norm(ALPHA * x + modp_ref[0, 5:6, :] * f_ref[...], ln2g_ref[...], ln2b_ref[...])
    shift1 = mod_ref[0, 0:1, :]
    scale1 = mod_ref[0, 1:2, :]
    gate1 = mod_ref[0, 2:3, :]
    u = (x * (1.0 + scale1) + shift1).astype(_BF16)

    qkv = _dot(u, win_ref[:, _Q0:_LX0])
    q_scr[...] = (qkv[:, 0:ATTN_W] * (HEAD_DIM ** -0.5)).astype(_BF16)
    low = lax.broadcasted_iota(jnp.int32, (ts, KV_W), 1) < HEAD_DIM
    k_new = qkv[:, _K0:_K0 + KV_W]
    k_rolled = pltpu.roll(k_new, HEAD_DIM, 1)
    v_new = qkv[:, _V0:_V0 + KV_W]
    v_rolled = pltpu.roll(v_new, HEAD_DIM, 1)
    k_scr[0, BLOCK:BLOCK + ts, :] = jnp.where(low, k_new, 0.0).astype(_BF16)
    k_scr[1, BLOCK:BLOCK + ts, :] = jnp.where(low, 0.0, k_rolled).astype(_BF16)
    k_scr[2, BLOCK:BLOCK + ts, :] = jnp.where(low, k_rolled, 0.0).astype(_BF16)
    k_scr[3, BLOCK:BLOCK + ts, :] = jnp.where(low, 0.0, k_new).astype(_BF16)
    ones_lo = jnp.where(low, 1.0, 0.0).astype(_BF16)
    ones_hi = jnp.where(low, 0.0, 1.0).astype(_BF16)
    v_scr[0, BLOCK:BLOCK + ts, 0:KV_W] = jnp.where(low, v_new, 0.0).astype(_BF16)
    v_scr[1, BLOCK:BLOCK + ts, 0:KV_W] = jnp.where(low, 0.0, v_rolled).astype(_BF16)
    v_scr[2, BLOCK:BLOCK + ts, 0:KV_W] = jnp.where(low, v_rolled, 0.0).astype(_BF16)
    v_scr[3, BLOCK:BLOCK + ts, 0:KV_W] = jnp.where(low, 0.0, v_new).astype(_BF16)
    for slot in range(4):
        v_scr[slot, BLOCK:BLOCK + ts, KV_W:2 * KV_W] = ones_lo if slot % 2 == 0 else ones_hi

    prev_cols = lax.broadcasted_iota(jnp.int32, (BLOCK, 2 * BLOCK), 1) < BLOCK
    low_o = lax.broadcasted_iota(jnp.int32, (BLOCK, 2 * HEAD_DIM), 1) < HEAD_DIM

    def attn_block(jb, carry):
        r0 = pl.multiple_of(jb * BLOCK, BLOCK)
        first = (i * nblk + jb) == 0
        neg = jnp.where(jnp.logical_and(prev_cols, first), _NEG_INF, 0.0)
        for p in range(N_PAIRS):
            g = p // PAIRS_PER_KV
            qp = q_scr[pl.ds(r0, BLOCK), p * 128:(p + 1) * 128]
            acc = None
            sink_terms = []
            for hh in range(2):
                s = (_dot_nt(qp, k_scr[2 * g + hh, pl.ds(r0, 2 * BLOCK), :])
                     + bias_ref[2 * p + hh] + neg)
                sink = sink_ref[layer, 2 * p + hh]
                m = jnp.maximum(jnp.max(s, axis=-1, keepdims=True), sink)
                e = jnp.exp(s - m).astype(_BF16)
                sink_terms.append(jnp.exp(sink - m))
                part = _dot(e, v_scr[2 * g + hh, pl.ds(r0, 2 * BLOCK), :])
                acc = part if acc is None else acc + part
            den = acc[:, 2 * HEAD_DIM:] + jnp.where(low_o, sink_terms[0], sink_terms[1])
            ya_scr[pl.ds(r0, BLOCK), p * 128:(p + 1) * 128] = acc[:, 0:2 * HEAD_DIM] / den
        return carry

    lax.fori_loop(0, nblk, attn_block, 0)

    k_scr[:, 0:BLOCK, :] = k_scr[:, ts:ts + BLOCK, :]
    v_scr[:, 0:BLOCK, :] = v_scr[:, ts:ts + BLOCK, :]

    lx = _dot(u, win_ref[:, _LX0:_LG0])
    ext = jnp.concatenate([lxprev_scr[...], lx], axis=0)
    lxprev_scr[...] = lx[ts - 8:ts, :]
    xc = convb_ref[...] + convw_ref[CONV_W - 1:CONV_W, :] * lx
    for j in range(1, CONV_W):
        shifted = pltpu.roll(ext, j, 0)[8:8 + ts, :]
        xc = xc + convw_ref[CONV_W - 1 - j:CONV_W - j, :] * shifted
    xcb = xc.astype(_BF16)
    r_parts = []
    i_parts = []
    for hb in range(LRU_BLOCKS):
        ri = _dot(xcb[:, hb * LRU_BW:(hb + 1) * LRU_BW], wax_ref[hb])
        r_parts.append(ri[:, 0:LRU_BW])
        i_parts.append(ri[:, LRU_BW:2 * LRU_BW])
    r = jax.nn.sigmoid(jnp.concatenate(r_parts, axis=1) + bax_ref[0:1, :])
    ig = jax.nn.sigmoid(jnp.concatenate(i_parts, axis=1) + bax_ref[1:2, :])
    nlam = -lam_ref[...]
    softplus = jnp.maximum(nlam, 0.0) + jnp.log(1.0 + jnp.exp(-jnp.abs(nlam)))
    a = jnp.exp((-LRU_C) * r * softplus)
    om = 1.0 - a * a
    root = jnp.where(om > 0.0, om * lax.rsqrt(om), 0.0)
    bv = root * (ig * xc)

    row8 = lax.broadcasted_iota(jnp.int32, (ts, LRU_W), 0) % 8
    for d in (1, 2, 4):
        a_s = pltpu.roll(a, d, 0)
        b_s = pltpu.roll(bv, d, 0)
        inside = row8 >= d
        bv = jnp.where(inside, a * b_s + bv, bv)
        a = jnp.where(inside, a * a_s, a)
    h = h_scr[...]
    h_parts = []
    for g8 in range(ts // 8):
        hg = bv[8 * g8:8 * g8 + 8, :] + a[8 * g8:8 * g8 + 8, :] * h
        h_parts.append(hg)
        h = hg[7:8, :]
    h_scr[...] = h
    hseq = jnp.concatenate(h_parts, axis=0)

    yb = hseq * jax.nn.gelu(_dot(u, win_ref[:, _LG0:_GA0]))

    ga = jax.nn.sigmoid(_dot(u, win_ref[:, _GA0:_GB0]))
    gb = jax.nn.sigmoid(_dot(u, win_ref[:, _GB0:IN_W]))
    y = (ga * ya_scr[...] + gb * yb).astype(_BF16)
    z = ALPHA * x + gate1 * _dot(y, wout_ref[...])
    o_ref[0] = _layer_norm(z, lng_ref[...], lnb_ref[...])


def _mix(layer, x, moe_out, mod_all, sink, win, bias, convw, convb, wax, bax, lam, wout, lng, lnb,
         ln2g, ln2b):
    b, s, d = x.shape
    ts = MIX_TS
    per_b = s // ts
    lay2 = lambda bi, i, sk: (layer, 0, 0)
    lay3 = lambda bi, i, sk: (layer, 0, 0, 0)
    prev2 = lambda bi, i, sk: (layer - 1, 0, 0)
    once = pl.Buffered(1)
    in_specs = [pl.BlockSpec((1, ts, d), lambda bi, i, sk: (bi, i, 0))]
    operands = [x]
    if layer > 0:
        in_specs += [
            pl.BlockSpec((ts, d), lambda bi, i, sk: (bi * per_b + i, 0)),
            pl.BlockSpec((None, 1, 6, d), lambda bi, i, sk: (layer - 1, bi, 0, 0)),
            pl.BlockSpec((None, 1, d), prev2),
            pl.BlockSpec((None, 1, d), prev2),
        ]
        operands += [moe_out, mod_all, ln2g, ln2b]
    in_specs += [
        pl.BlockSpec((None, 1, 6, d), lambda bi, i, sk: (layer, bi, 0, 0)),
        pl.BlockSpec((None, d, IN_W), lay2, pipeline_mode=once),
        pl.BlockSpec((N_Q_HEADS, BLOCK, 2 * BLOCK), lambda bi, i, sk: (0, 0, 0), pipeline_mode=once),
        pl.BlockSpec((None, CONV_W, d), lay2),
        pl.BlockSpec((None, 1, d), lay2),
        pl.BlockSpec((None, LRU_BLOCKS, LRU_BW, 2 * LRU_BW), lay3),
        pl.BlockSpec((None, 2, d), lay2),
        pl.BlockSpec((None, 1, d), lay2),
        pl.BlockSpec((None, d, d), lay2, pipeline_mode=once),
        pl.BlockSpec((None, 1, d), lay2),
        pl.BlockSpec((None, 1, d), lay2),
    ]
    operands += [mod_all, win, bias, convw, convb, wax, bax, lam, wout, lng, lnb]
    grid_spec = pltpu.PrefetchScalarGridSpec(
        num_scalar_prefetch=1,
        grid=(b, per_b),
        in_specs=in_specs,
        out_specs=pl.BlockSpec((1, ts, d), lambda bi, i, sk: (bi, i, 0)),
        scratch_shapes=[
            pltpu.VMEM((ts, ATTN_W), _BF16),
            pltpu.VMEM((4, ts + BLOCK, KV_W), _BF16),
            pltpu.VMEM((4, ts + BLOCK, 2 * KV_W), _BF16),
            pltpu.VMEM((ts, ATTN_W), _F32),
            pltpu.VMEM((8, LRU_W), _F32),
            pltpu.VMEM((1, LRU_W), _F32),
        ],
    )
    return pl.pallas_call(
        functools.partial(_mix_kernel, layer=layer),
        grid_spec=grid_spec,
        out_shape=jax.ShapeDtypeStruct((b, s, d), _F32),
        name="mix",
        compiler_params=pltpu.CompilerParams(
            dimension_semantics=("arbitrary", "arbitrary"), vmem_limit_bytes=VMEM_LIMIT),
    )(sink, *operands)


def _second_largest(v0, v1, v2, v3):
    hi1, lo1 = jnp.maximum(v0, v1), jnp.minimum(v0, v1)
    hi2, lo2 = jnp.maximum(v2, v3), jnp.minimum(v2, v3)
    return jnp.maximum(hi1, hi2), jnp.maximum(jnp.minimum(hi1, hi2), jnp.maximum(lo1, lo2))


def _route_kernel(x_ref, mod_ref, rwt_ref, rb_ref, tri_ref, row_ref, pos_ref, cnt_ref, cnt_scr,
                  *, n_tokens):
    tm = x_ref.shape[0]
    j = pl.program_id(0)

    @pl.when(j == 0)
    def _():
        cnt_scr[...] = jnp.zeros_like(cnt_scr)

    shift2 = mod_ref[0, 3:4, :]
    scale2 = mod_ref[0, 4:5, :]
    u = x_ref[...] * (1.0 + scale2) + shift2
    row_ref[:, 0:D_MODEL] = u

    logits = _dot_nt(rwt_ref[...], u, precision=lax.Precision.HIGHEST) + rb_ref[...]
    mx = jnp.max(logits, axis=0, keepdims=True)
    ex = jnp.exp(logits - mx)
    scores = ex / jnp.sum(ex, axis=0, keepdims=True)
    sc = [scores[e:e + 1, :] for e in range(N_EXPERTS)]

    gscore = []
    for g in range(N_GROUPS):
        top1, top2 = _second_largest(*sc[4 * g:4 * g + 4])
        gscore.append(top1 + top2)
    sel = []
    taken = None
    for g in range(N_GROUPS):
        best = None
        for g2 in range(g + 1, N_GROUPS):
            c = gscore[g] >= gscore[g2]
            best = c if best is None else jnp.logical_and(best, c)
        if best is None:
            best = jnp.ones_like(gscore[g], dtype=jnp.bool_)
        if taken is not None:
            best = jnp.logical_and(best, jnp.logical_not(taken))
        sel.append(best)
        taken = best if taken is None else jnp.logical_or(taken, best)

    vals = []
    for k in range(EXPERTS_PER_GROUP):
        v = sc[12 + k]
        for g in (2, 1, 0):
            v = jnp.where(sel[g], sc[4 * g + k], v)
        vals.append(v)
    ranks = []
    for k in range(EXPERTS_PER_GROUP):
        rk = jnp.zeros_like(vals[k])
        for k2 in range(EXPERTS_PER_GROUP):
            if k2 == k:
                continue
            beats = (vals[k2] >= vals[k]) if k2 < k else (vals[k2] > vals[k])
            rk = rk + jnp.where(beats, 1.0, 0.0)
        ranks.append(rk)
    top_a = jnp.zeros_like(vals[0])
    top_b = jnp.zeros_like(vals[0])
    for k in range(EXPERTS_PER_GROUP):
        top_a = jnp.where(ranks[k] == 0.0, vals[k], top_a)
        top_b = jnp.where(ranks[k] == 1.0, vals[k], top_b)
    denom = top_a + top_b
    cw = [jnp.where(ranks[k] < 2.0, vals[k] / denom, 0.0) for k in range(EXPERTS_PER_GROUP)]

    sub = lax.broadcasted_iota(jnp.int32, (8, tm), 0)
    cw8 = jnp.zeros((8, tm), _F32)
    for k in range(EXPERTS_PER_GROUP):
        cw8 = jnp.where(sub == k, cw[k], cw8)
    meta_t = jnp.concatenate([cw8, jnp.zeros((META_W - 8, tm), _F32)], axis=0)
    row_ref[:, D_MODEL:ROW_W] = meta_t.T

    onehot = jnp.zeros((8, tm), _F32)
    for g in range(N_GROUPS):
        onehot = jnp.where(jnp.logical_and(sub == g, sel[g]), 1.0, onehot)
    before = _dot(onehot.astype(_BF16), tri_ref[...])
    cnt = cnt_scr[...]
    rank = jnp.sum(onehot * (before + cnt[:, 0:1]), axis=0, keepdims=True)
    gidx = jnp.sum(onehot * sub.astype(_F32), axis=0, keepdims=True)
    pos_ref[0, 0:1, :] = gidx.astype(jnp.int32)
    pos_ref[0, 1:2, :] = rank.astype(jnp.int32)
    cnt = cnt + jnp.sum(onehot, axis=1, keepdims=True)
    cnt_scr[...] = cnt
    cnt_ref[...] = cnt.astype(jnp.int32)


def _route(x1, mod, rwt, rb, tri, tokens_per_batch):
    t, d = x1.shape
    tm = ROUTE_TM
    per_b = tokens_per_batch // tm
    return pl.pallas_call(
        functools.partial(_route_kernel, n_tokens=t),
        grid=(t // tm,),
        in_specs=[pl.BlockSpec((tm, d), lambda j: (j, 0)),
                  pl.BlockSpec((1, 6, d), lambda j: (j // per_b, 0, 0)),
                  pl.BlockSpec((N_EXPERTS, d), lambda j: (0, 0)),
                  pl.BlockSpec((N_EXPERTS, 1), lambda j: (0, 0)),
                  pl.BlockSpec((tm, tm), lambda j: (0, 0))],
        out_specs=[pl.BlockSpec((tm, ROW_W), lambda j: (j, 0)),
                   pl.BlockSpec((1, 2, tm), lambda j: (j, 0, 0)),
                   pl.BlockSpec((8, 128), lambda j: (0, 0))],
        out_shape=[jax.ShapeDtypeStruct((t, ROW_W), _F32),
                   jax.ShapeDtypeStruct((t // tm, 2, tm), jnp.int32),
                   jax.ShapeDtypeStruct((8, 128), jnp.int32)],
        scratch_shapes=[pltpu.VMEM((8, 128), _F32)],
        name="route",
        compiler_params=pltpu.CompilerParams(
            dimension_semantics=("arbitrary",), vmem_limit_bytes=VMEM_LIMIT),
    )(x1, mod, rwt, rb, tri)


def _slots_kernel(grp_ref, rank_ref, cnt_ref, src_ref, base_scr, *, n_tokens):
    tm = EXPERT_TM

    def fill(s, carry):
        src_ref[s] = n_tokens + lax.rem(s, TRASH_ROWS)
        return carry

    base = jnp.int32(0)
    for g in range(N_GROUPS):
        c = cnt_ref[g]
        cap = ((c + (tm - 1)) // tm) * tm
        base_scr[g] = base
        lax.fori_loop(base + c, base + cap, fill, 0)
        base = base + cap
    lax.fori_loop(base, base + tm, fill, 0)

    def place(t, carry):
        src_ref[base_scr[grp_ref[t]] + rank_ref[t]] = t
        return carry

    lax.fori_loop(0, n_tokens, place, 0, unroll=16)


def _slots(grp_of_token, rank_of_token, cnt):
    t = grp_of_token.shape[0]
    grid_spec = pltpu.PrefetchScalarGridSpec(
        num_scalar_prefetch=3,
        grid=(1,),
        in_specs=[],
        out_specs=pl.BlockSpec(memory_space=pltpu.SMEM),
        scratch_shapes=[pltpu.SMEM((N_GROUPS,), jnp.int32)],
    )
    return pl.pallas_call(
        functools.partial(_slots_kernel, n_tokens=t),
        grid_spec=grid_spec,
        out_shape=jax.ShapeDtypeStruct((t + TRASH_ROWS,), jnp.int32),
        name="slots",
        compiler_params=pltpu.CompilerParams(dimension_semantics=("arbitrary",)),
    )(grp_of_token, rank_of_token, cnt)


def _expert_kernel(src_ref, grp_ref, live_ref, rows_ref, wg_ref, wu_ref, wd_ref, f_ref,
                   x_even, x_odd, y_even, y_odd, gsem, ssem, *, n_tokens):
    tm = x_even.shape[0]
    j = pl.program_id(0)
    x_bufs = (x_even, x_odd)
    y_bufs = (y_even, y_odd)

    def gather_row(tile, par, r):
        tok = jnp.minimum(src_ref[tile * tm + r], n_tokens - 1)
        return pltpu.make_async_copy(rows_ref.at[pl.ds(tok, 1), :], x_bufs[par].at[pl.ds(r, 1), :],
                                     gsem.at[par])

    def scatter_row(tile, par, r):
        dst = jnp.where(tile < 0, n_tokens + r, src_ref[jnp.maximum(tile, 0) * tm + r])
        return pltpu.make_async_copy(y_bufs[par].at[pl.ds(r, 1), :], f_ref.at[pl.ds(dst, 1), :],
                                     ssem.at[par])

    def start_gather(tile, par):
        for r in range(tm):
            gather_row(tile, par, r).start()

    def wait_gather(par):
        for r in range(tm):
            gather_row(0, par, r).wait()

    def start_scatter(tile, par):
        for r in range(tm):
            scatter_row(tile, par, r).start()

    def wait_scatter(par):
        for r in range(tm):
            scatter_row(0, par, r).wait()

    live = live_ref[j] != 0
    prev_live = jnp.logical_and(j >= 1, live_ref[jnp.maximum(j - 1, 0)] != 0)

    @pl.when(j == 0)
    def _():
        start_gather(0, 0)
        y_odd[...] = jnp.zeros_like(y_odd)

    for par in range(2):
        other = 1 - par
        mine = (j % 2) == par

        @pl.when(jnp.logical_and(live, mine))
        def _(par=par, other=other):
            wait_gather(par)

            @pl.when(j >= 1)
            def _():
                wait_scatter(par)

            @pl.when(j >= 0)
            def _():
                for r in range(tm):
                    gather_row(j + 1, other, r).start()
                    scatter_row(j - 1, other, r).start()

            x_ref = x_bufs[par]
            x = x_ref[:, 0:D_MODEL].astype(_BF16)
            acc = None
            for e in range(EXPERTS_PER_GROUP):
                gate = _dot(x, wg_ref[e])
                up = _dot(x, wu_ref[e])
                h = gate * jax.nn.sigmoid(gate) * up * x_ref[:, D_MODEL + e:D_MODEL + e + 1]
                part = _dot(h.astype(_BF16), wd_ref[e])
                acc = part if acc is None else acc + part
            y_bufs[par][...] = acc

        @pl.when(jnp.logical_and(jnp.logical_and(jnp.logical_not(live), prev_live), mine))
        def _(par=par, other=other):
            wait_gather(par)
            wait_scatter(par)
            start_scatter(j - 1, other)
            wait_scatter(other)


def _experts(layer, src, grp, live, rows, wg, wu, wd):
    tm = EXPERT_TM
    t = rows.shape[0]
    n_tiles = grp.shape[0]
    grid_spec = pltpu.PrefetchScalarGridSpec(
        num_scalar_prefetch=3,
        grid=(n_tiles,),
        in_specs=[
            pl.BlockSpec(memory_space=pl.ANY),
            pl.BlockSpec((None, EXPERTS_PER_GROUP, D_MODEL, EXPERT_FF),
                         lambda j, s, g, v: (layer, g[j], 0, 0)),
            pl.BlockSpec((None, EXPERTS_PER_GROUP, D_MODEL, EXPERT_FF),
                         lambda j, s, g, v: (layer, g[j], 0, 0)),
            pl.BlockSpec((None, EXPERTS_PER_GROUP, EXPERT_FF, D_MODEL),
                         lambda j, s, g, v: (layer, g[j], 0, 0)),
        ],
        out_specs=pl.BlockSpec(memory_space=pl.ANY),
        scratch_shapes=[
            pltpu.VMEM((tm, ROW_W), _F32),
            pltpu.VMEM((tm, ROW_W), _F32),
            pltpu.VMEM((tm, D_MODEL), _F32),
            pltpu.VMEM((tm, D_MODEL), _F32),
            pltpu.SemaphoreType.DMA((2,)),
            pltpu.SemaphoreType.DMA((2,)),
        ],
    )
    return pl.pallas_call(
        functools.partial(_expert_kernel, n_tokens=t),
        grid_spec=grid_spec,
        out_shape=jax.ShapeDtypeStruct((t + TRASH_ROWS, D_MODEL), _F32),
        name="experts",
        compiler_params=pltpu.CompilerParams(
            dimension_semantics=("arbitrary",), vmem_limit_bytes=VMEM_LIMIT),
    )(src, grp, live, rows, wg, wu, wd)


def _tile_plan(cnt, n_tokens):
    tm = EXPERT_TM
    n_tiles = n_tokens // tm + N_GROUPS
    ends = jnp.cumsum((cnt + tm - 1) // tm)
    j = jnp.arange(n_tiles, dtype=jnp.int32)
    grp = jnp.minimum(jnp.sum((j[:, None] >= ends[None, :]).astype(jnp.int32), axis=1), N_GROUPS - 1)
    live = (j < ends[-1]).astype(jnp.int32)
    return grp.astype(jnp.int32), live


def _final_norm_kernel(x_ref, f_ref, mod_ref, lng_ref, lnb_ref, o_ref):
    z = ALPHA * x_ref[...] + mod_ref[0, 5:6, :] * f_ref[...]
    o_ref[...] = _layer_norm(z, lng_ref[...], lnb_ref[...])


def _final_norm(x1, moe_out, mod, lng, lnb, tokens_per_batch):
    t, d = x1.shape
    tf = COMBINE_TF
    per_b = tokens_per_batch // tf
    return pl.pallas_call(
        _final_norm_kernel,
        grid=(t // tf,),
        in_specs=[pl.BlockSpec((tf, d), lambda j: (j, 0)),
                  pl.BlockSpec((tf, d), lambda j: (j, 0)),
                  pl.BlockSpec((1, 6, d), lambda j: (j // per_b, 0, 0)),
                  pl.BlockSpec((1, d), lambda j: (0, 0)),
                  pl.BlockSpec((1, d), lambda j: (0, 0))],
        out_specs=pl.BlockSpec((tf, d), lambda j: (j, 0)),
        out_shape=jax.ShapeDtypeStruct((t, d), _F32),
        name="final_norm",
        compiler_params=pltpu.CompilerParams(
            dimension_semantics=("arbitrary",), vmem_limit_bytes=VMEM_LIMIT),
    )(x1, moe_out, mod, lng, lnb)


def _t5_causal_bucket(dist):
    max_exact = N_BUCKETS // 2
    d = np.maximum(dist, 0)
    df = np.maximum(d, 1).astype(np.float32)
    large = max_exact + (np.log(df / max_exact) / math.log(MAX_DISTANCE / max_exact)
                         * (N_BUCKETS - max_exact)).astype(np.int32)
    large = np.minimum(large, N_BUCKETS - 1)
    return np.where(d < max_exact, d, large).astype(np.int32)


def _head_bias(rel_bias):
    qi = np.arange(BLOCK)[:, None]
    sj = np.arange(2 * BLOCK)[None, :]
    dist = qi + BLOCK - sj
    in_window = (dist >= 0) & (dist < WINDOW)
    per_dist = rel_bias.astype(_F32)[_t5_causal_bucket(np.arange(WINDOW))]
    onehot = (np.clip(dist, 0, WINDOW - 1).reshape(-1)[:, None] == np.arange(WINDOW)[None, :])
    bias = jnp.dot(jnp.asarray(onehot, _F32), per_dist, precision=lax.Precision.HIGHEST)
    bias = jnp.where(in_window.reshape(-1)[:, None], bias, _NEG_INF)
    return jnp.transpose(bias).reshape(N_Q_HEADS, BLOCK, 2 * BLOCK)


def kernel(x, c, ada_w, ada_b, w_in, attn_sink, rel_bias, conv_w, conv_b, lru_wa, lru_ba, lru_wx,
           lru_bx, lru_lambda, w_out, ln1_g, ln1_b, router_w, router_b, moe_w_gate, moe_w_up,
           moe_w_down, ln2_g, ln2_b):
    b, s, d = x.shape
    t = b * s
    depth = w_in.shape[0]

    c_pad = jnp.pad(c, ((0, 8 - b), (0, 0)))
    mod_all = _modulation(c_pad, ada_w, ada_b)
    mod_all = mod_all[:, :b, :].reshape(depth, b, 6, d)

    biasp = _head_bias(rel_bias)
    rwt = router_w.T
    rb = router_b.reshape(N_EXPERTS, 1)
    tri = jnp.asarray(np.triu(np.ones((ROUTE_TM, ROUTE_TM), np.float32), 1), _BF16)

    win_bf = w_in.astype(_BF16)
    wout_bf = w_out.astype(_BF16)
    wax = jnp.concatenate([lru_wa, lru_wx], axis=-1).astype(_BF16)
    bax = jnp.stack([lru_ba, lru_bx], axis=1)
    wg_bf = moe_w_gate.astype(_BF16)
    wu_bf = moe_w_up.astype(_BF16)
    wd_bf = moe_w_down.astype(_BF16)
    conv_b3 = conv_b.reshape(depth, 1, d)
    lam3 = lru_lambda.reshape(depth, 1, d)
    ln1_g3 = ln1_g.reshape(depth, 1, d)
    ln1_b3 = ln1_b.reshape(depth, 1, d)

    ln2_g3 = ln2_g.reshape(depth, 1, d)
    ln2_b3 = ln2_b.reshape(depth, 1, d)

    moe_out = None
    for l in range(depth):
        x = _mix(l, x, moe_out, mod_all, attn_sink, win_bf, biasp, conv_w, conv_b3, wax, bax, lam3,
                 wout_bf, ln1_g3, ln1_b3, ln2_g3, ln2_b3)
        rows, pos, cnt = _route(x.reshape(t, d), mod_all[l], rwt, rb, tri, s)
        cnt = cnt[:N_GROUPS, 0]
        src = _slots(pos[:, 0, :].reshape(t), pos[:, 1, :].reshape(t), cnt)
        grp, live = _tile_plan(cnt, t)
        moe_out = _experts(l, src, grp, live, rows, wg_bf, wu_bf, wd_bf)
    out = _final_norm(x.reshape(t, d), moe_out, mod_all[depth - 1], ln2_g3[depth - 1], ln2_b3[depth - 1], s)
    return out.reshape(b, s, d)
```
